```python
import math
import functools
import jax
import jax.numpy as jnp
from jax import lax
import numpy as np

D_MODEL = 1024
BATCH = 16
SEQ = 2048
DEPTH = 2
DEC_BATCH = 32
DEC_SEQ = 4
PAST_LEN = 16384
PAGE_SIZE = 128

SB_HEAD_DIM = 64
SB_HEADS = D_MODEL // 128
SB_WIDTH = SB_HEADS * SB_HEAD_DIM
SB_BLOCK = 128
SB_BIAS_INIT = -8.0
LRU_WIDTH = D_MODEL // 2
LRU_BLOCKS = 8
LRU_BLOCK_DIM = LRU_WIDTH // LRU_BLOCKS
CONV_W = 4
LRU_C = 8.0
GLA_HEADS = 4
GLA_DK = D_MODEL // 16
GLA_DV = D_MODEL // 8
GLA_KW = GLA_HEADS * GLA_DK
GLA_VW = GLA_HEADS * GLA_DV
GLA_RANK = 16
GLA_TAU = 16.0
GLA_CHUNK = 16
N_BRANCH = 3
D_FF = -(-8 * D_MODEL // (3 * 256)) * 256
IN_SIZES = (SB_WIDTH, SB_WIDTH, SB_WIDTH, LRU_WIDTH, LRU_WIDTH,
            GLA_KW, GLA_KW, GLA_VW, GLA_VW, GLA_RANK, D_MODEL, D_MODEL, D_MODEL)
IN_WIDTH = sum(IN_SIZES)
NORM_EPS = 1e-6

kernel_name = 'hybrid_stickbreak_rglru_gla_step'


def _rmsnorm(x, g):
    xf = x.astype(jnp.float32)
    y = xf * lax.rsqrt(jnp.mean(xf * xf, axis=-1, keepdims=True) + NORM_EPS)
    return (y * g.astype(jnp.float32)).astype(x.dtype)


def _sb_attend(q, k, v, q_pos, k_pos, bias):
    z = (jnp.einsum('bqhd,bkhd->bhqk', q, k).astype(jnp.float32) * (SB_HEAD_DIM ** -0.5)
         + bias.astype(jnp.float32)[None, :, None, None])
    mask = k_pos[None, :] < q_pos[:, None]
    log_keep = jnp.where(mask, jax.nn.log_sigmoid(-z), 0.0)
    suffix = lax.cumsum(log_keep, axis=3, reverse=True) - log_keep
    a = jnp.where(mask, jnp.exp(jax.nn.log_sigmoid(z) + suffix), 0.0)
    return jnp.einsum('bhqk,bkhd->bqhd', a, v.astype(jnp.float32)).astype(v.dtype)


def _sb_prompt(q, k, v, bias):
    T = q.shape[1]
    outs = []
    for blk in range(T // SB_BLOCK):
        s, e = blk * SB_BLOCK, (blk + 1) * SB_BLOCK
        outs.append(_sb_attend(q[:, s:e], k[:, :e], v[:, :e], jnp.arange(s, e), jnp.arange(e), bias))
    return jnp.concatenate(outs, axis=1)


def _sb_sample(q, k, v, bias, k_past, v_past):
    P, T = k_past.shape[1], q.shape[1]
    k_all = jnp.concatenate([k_past.astype(k.dtype), k], axis=1)
    v_all = jnp.concatenate([v_past.astype(v.dtype), v], axis=1)
    return _sb_attend(q, k_all, v_all, P + jnp.arange(T), jnp.arange(P + T), bias)


def _lin_combine(e1, e2):
    a1, b1 = e1
    a2, b2 = e2
    return a1 * a2, a2 * b1 + b2


def _rglru(xr, gr, conv_buf, h0, conv_w, conv_b, wa, ba, wx, bx, lam):
    B, T, W = xr.shape
    xpad = jnp.concatenate([conv_buf.astype(xr.dtype), xr], axis=1)
    xc = conv_b + xpad[:, 0:T] * conv_w[0]
    for w in range(1, CONV_W):
        xc = xc + xpad[:, w:w + T] * conv_w[w]
    xb = xc.reshape(B, T, LRU_BLOCKS, LRU_BLOCK_DIM)
    r = jax.nn.sigmoid((jnp.einsum('btnj,njk->btnk', xb, wa).reshape(B, T, W) + ba).astype(jnp.float32))
    i = jax.nn.sigmoid((jnp.einsum('btnj,njk->btnk', xb, wx).reshape(B, T, W) + bx).astype(jnp.float32))
    log_a = -LRU_C * r * jax.nn.softplus(-lam.astype(jnp.float32))
    a = jnp.exp(log_a)
    b = jnp.sqrt(-jnp.expm1(2.0 * log_a)) * (i * xc.astype(jnp.float32))
    b = b.at[:, 0].add(a[:, 0] * h0.astype(jnp.float32))
    _, hs = lax.associative_scan(_lin_combine, (a, b), axis=1)
    y = (hs * jax.nn.gelu(gr.astype(jnp.float32))).astype(xr.dtype)
    return y, xpad[:, T:], hs[:, -1].astype(xr.dtype)


def _gla_chunked(q, k, v, la, S0):
    B, T, H, DK = q.shape
    DV = v.shape[-1]
    C = math.gcd(T, GLA_CHUNK)
    n = T // C

    def to_chunks(a):
        return jnp.moveaxis(a.reshape(B, n, C, *a.shape[2:]), 1, 0)

    causal = jnp.tril(jnp.ones((C, C), dtype=bool))

    def step(S, inp):
        qc, kc, vc, lac = inp
        G = jnp.cumsum(lac, axis=1)
        G_last = G[:, -1:]
        o_inter = jnp.einsum('bchk,bhkv->bchv', qc * jnp.exp(G), S)
        k_dec = kc * jnp.exp(G_last - G)
        q_rel = qc * jnp.exp(G - G_last)
        att = jnp.where(causal, jnp.einsum('bthk,bshk->bhts', q_rel, k_dec), 0.0)
        o_intra = jnp.einsum('bhts,bshv->bthv', att, vc)
        S_new = jnp.exp(G_last[:, 0])[..., None] * S + jnp.einsum('bshk,bshv->bhkv', k_dec, vc)
        return S_new, o_inter + o_intra

    S_fin, o = lax.scan(step, S0, (to_chunks(q), to_chunks(k), to_chunks(v), to_chunks(la)))
    return jnp.moveaxis(o, 0, 1).reshape(B, T, H, DV), S_fin


def _gla(gq, gk, gv, go, glr, S0, w_alpha, b_alpha, g_norm):
    B, T, _ = gq.shape
    f32 = jnp.float32
    q = gq.reshape(B, T, GLA_HEADS, GLA_DK).astype(f32) * (GLA_DK ** -0.5)
    k = gk.reshape(B, T, GLA_HEADS, GLA_DK).astype(f32)
    v = gv.reshape(B, T, GLA_HEADS, GLA_DV).astype(f32)
    la = jax.nn.log_sigmoid((jnp.einsum('btr,rk->btk', glr, w_alpha) + b_alpha).astype(f32)) / GLA_TAU
    la = la.reshape(B, T, GLA_HEADS, GLA_DK)
    o, S = _gla_chunked(q, k, v, la, S0.astype(f32))
    o = o * lax.rsqrt(jnp.mean(o * o, axis=-1, keepdims=True) + NORM_EPS)
    o = o * g_norm.astype(f32).reshape(GLA_HEADS, GLA_DV)
    y = o.reshape(B, T, GLA_VW) * jax.nn.silu(go.astype(f32))
    return y.astype(gq.dtype), S.astype(gq.dtype)


def _layer(x, c, lp, sb_fn, conv_buf, h0, S0):
    B, T, _ = x.shape
    mod = jnp.einsum('bd,de->be', jax.nn.silu(c), lp['w_ada']) + lp['b_ada']
    sh_m, sc_m, g_m, sh_f, sc_f, g_f = jnp.split(mod[:, None, :], 6, axis=-1)
    h = _rmsnorm(x, lp['norm_mix']) * (1.0 + sc_m) + sh_m
    proj = jnp.einsum('btd,dc->btc', h, lp['w_in'])
    splits = [int(s) for s in np.cumsum(IN_SIZES)[:-1]]
    sq, sk, sv, xr, gr, gq, gk, gv, go, glr, m_sb, m_lru, m_gla = jnp.split(proj, splits, axis=-1)
    q = sq.reshape(B, T, SB_HEADS, SB_HEAD_DIM)
    k = sk.reshape(B, T, SB_HEADS, SB_HEAD_DIM)
    v = sv.reshape(B, T, SB_HEADS, SB_HEAD_DIM)
    y_sb = sb_fn(q, k, v, lp['sb_bias']).reshape(B, T, SB_WIDTH)
    y_lru, conv_new, h_last = _rglru(xr, gr, conv_buf, h0, lp['conv_w'], lp['conv_b'],
                                     lp['lru_wa'], lp['lru_ba'], lp['lru_wx'], lp['lru_bx'], lp['lru_lambda'])
    y_gla, S_last = _gla(gq, gk, gv, go, glr, S0, lp['gla_w_alpha'], lp['gla_b_alpha'], lp['gla_norm'])
    merged = (jax.nn.sigmoid(m_sb) * jnp.einsum('btw,wd->btd', y_sb, lp['w_br_sb'])
              + jax.nn.sigmoid(m_lru) * jnp.einsum('btw,wd->btd', y_lru, lp['w_br_lru'])
              + jax.nn.sigmoid(m_gla) * jnp.einsum('btw,wd->btd', y_gla, lp['w_br_gla']))
    x = x + g_m * jnp.einsum('btd,de->bte', merged, lp['w_out'])
    h = _rmsnorm(x, lp['norm_ffn']) * (1.0 + sc_f) + sh_f
    ff = jax.nn.silu(jnp.einsum('btd,df->btf', h, lp['w_gate'])) * jnp.einsum('btd,df->btf', h, lp['w_up'])
    x = x + g_f * jnp.einsum('btf,fd->btd', ff, lp['w_down'])
    return x, (k, v, conv_new, h_last, S_last)


def setup_inputs(seed: int = 0) -> dict:
    key = jax.random.key(seed)
    ks = jax.random.split(key, 40)
    f32 = jnp.float32

    def nrm(k, shape, scale=1.0):
        return scale * jax.random.normal(k, shape, f32)

    n_pages = PAST_LEN // PAGE_SIZE
    n_used = DEC_BATCH * n_pages
    n_pool = n_used + n_used // 4
    page_table = jax.random.permutation(ks[6], n_pool)[:n_used].reshape(DEC_BATCH, n_pages).astype(jnp.int32)
    u = jax.random.uniform(ks[20], (DEPTH, LRU_WIDTH), f32, 0.9, 0.999)
    sig = u ** (1.0 / LRU_C)
    lru_lambda = jnp.log(sig) - jnp.log1p(-sig)
    return {
        'x_prompt': nrm(ks[0], (BATCH, SEQ, D_MODEL)),
        'x_sample': nrm(ks[1], (DEC_BATCH, DEC_SEQ, D_MODEL)),
        'c_prompt': nrm(ks[2], (BATCH, D_MODEL)),
        'c_sample': nrm(ks[3], (DEC_BATCH, D_MODEL)),
        'cache_k': nrm(ks[4], (DEPTH, n_pool, PAGE_SIZE, SB_HEADS, SB_HEAD_DIM)),
        'cache_v': nrm(ks[5], (DEPTH, n_pool, PAGE_SIZE, SB_HEADS, SB_HEAD_DIM)),
        'page_table': page_table,
        'state_conv': nrm(ks[7], (DEPTH, DEC_BATCH, CONV_W - 1, LRU_WIDTH)),
        'state_lru': nrm(ks[8], (DEPTH, DEC_BATCH, LRU_WIDTH), 0.5),
        'state_gla': nrm(ks[9], (DEPTH, DEC_BATCH, GLA_HEADS, GLA_DK, GLA_DV), 0.5),
        'w_ada': nrm(ks[10], (DEPTH, D_MODEL, 6 * D_MODEL), D_MODEL ** -0.5),
        'b_ada': nrm(ks[11], (DEPTH, 6 * D_MODEL), 0.01),
        'norm_mix': 1.0 + nrm(ks[12], (DEPTH, D_MODEL), 0.01),
        'w_in': nrm(ks[13], (DEPTH, D_MODEL, IN_WIDTH), D_MODEL ** -0.5),
        'sb_bias': SB_BIAS_INIT + nrm(ks[33], (DEPTH, SB_HEADS), 0.1),
        'conv_w': nrm(ks[14], (DEPTH, CONV_W, LRU_WIDTH), CONV_W ** -0.5),
        'conv_b': nrm(ks[15], (DEPTH, LRU_WIDTH), 0.01),
        'lru_wa': nrm(ks[16], (DEPTH, LRU_BLOCKS, LRU_BLOCK_DIM, LRU_BLOCK_DIM), LRU_BLOCK_DIM ** -0.5),
        'lru_ba': nrm(ks[17], (DEPTH, LRU_WIDTH), 0.01),
        'lru_wx': nrm(ks[18], (DEPTH, LRU_BLOCKS, LRU_BLOCK_DIM, LRU_BLOCK_DIM), LRU_BLOCK_DIM ** -0.5),
        'lru_bx': nrm(ks[19], (DEPTH, LRU_WIDTH), 0.01),
        'lru_lambda': lru_lambda,
        'gla_w_alpha': nrm(ks[21], (DEPTH, GLA_RANK, GLA_KW), GLA_RANK ** -0.5),
        'gla_b_alpha': nrm(ks[22], (DEPTH, GLA_KW), 0.01),
        'gla_norm': 1.0 + nrm(ks[23], (DEPTH, GLA_VW), 0.01),
        'w_br_sb': nrm(ks[24], (DEPTH, SB_WIDTH, D_MODEL), SB_WIDTH ** -0.5),
        'w_br_lru': nrm(ks[25], (DEPTH, LRU_WIDTH, D_MODEL), LRU_WIDTH ** -0.5),
        'w_br_gla': nrm(ks[26], (DEPTH, GLA_VW, D_MODEL), GLA_VW ** -0.5),
        'w_out': nrm(ks[27], (DEPTH, D_MODEL, D_MODEL), D_MODEL ** -0.5),
        'norm_ffn': 1.0 + nrm(ks[28], (DEPTH, D_MODEL), 0.01),
        'w_gate': nrm(ks[29], (DEPTH, D_MODEL, D_FF), D_MODEL ** -0.5),
        'w_up': nrm(ks[30], (DEPTH, D_MODEL, D_FF), D_MODEL ** -0.5),
        'w_down': nrm(ks[31], (DEPTH, D_FF, D_MODEL), D_FF ** -0.5),
        'norm_final': 1.0 + nrm(ks[32], (D_MODEL,), 0.01),
    }


def reference(x_prompt, x_sample, c_prompt, c_sample, cache_k, cache_v, page_table, state_conv, state_lru,
              state_gla, w_ada, b_ada, norm_mix, w_in, sb_bias, conv_w, conv_b, lru_wa, lru_ba, lru_wx, lru_bx,
              lru_lambda, gla_w_alpha, gla_b_alpha, gla_norm, w_br_sb, w_br_lru, w_br_gla, w_out, norm_ffn,
              w_gate, w_up, w_down, norm_final):
    xp, xs = x_prompt, x_sample
    Bp, Bs = xp.shape[0], xs.shape[0]
    kp_l, vp_l, cvp_l, hp_l, sp_l = [], [], [], [], []
    ks_l, vs_l, cvs_l, hs_l, ss_l = [], [], [], [], []
    for l in range(DEPTH):
        lp = {'w_ada': w_ada[l], 'b_ada': b_ada[l], 'norm_mix': norm_mix[l], 'w_in': w_in[l],
              'sb_bias': sb_bias[l],
              'conv_w': conv_w[l], 'conv_b': conv_b[l], 'lru_wa': lru_wa[l], 'lru_ba': lru_ba[l],
              'lru_wx': lru_wx[l], 'lru_bx': lru_bx[l], 'lru_lambda': lru_lambda[l],
              'gla_w_alpha': gla_w_alpha[l], 'gla_b_alpha': gla_b_alpha[l], 'gla_norm': gla_norm[l],
              'w_br_sb': w_br_sb[l], 'w_br_lru': w_br_lru[l], 'w_br_gla': w_br_gla[l], 'w_out': w_out[l],
              'norm_ffn': norm_ffn[l], 'w_gate': w_gate[l], 'w_up': w_up[l], 'w_down': w_down[l]}
        xp, (k1, v1, cv1, h1, s1) = _layer(
            xp, c_prompt, lp, _sb_prompt,
            jnp.zeros((Bp, CONV_W - 1, LRU_WIDTH), xp.dtype),
            jnp.zeros((Bp, LRU_WIDTH), xp.dtype),
            jnp.zeros((Bp, GLA_HEADS, GLA_DK, GLA_DV), jnp.float32))
        k_past = cache_k[l][page_table].reshape(Bs, -1, SB_HEADS, SB_HEAD_DIM)
        v_past = cache_v[l][page_table].reshape(Bs, -1, SB_HEADS, SB_HEAD_DIM)
        xs, (k2, v2, cv2, h2, s2) = _layer(
            xs, c_sample, lp, functools.partial(_sb_sample, k_past=k_past, v_past=v_past),
            state_conv[l], state_lru[l], state_gla[l])
        kp_l.append(k1); vp_l.append(v1); cvp_l.append(cv1); hp_l.append(h1); sp_l.append(s1)
        ks_l.append(k2); vs_l.append(v2); cvs_l.append(cv2); hs_l.append(h2); ss_l.append(s2)
    y_prompt = _rmsnorm(xp, norm_final)
    y_sample = _rmsnorm(xs, norm_final)
    return (y_prompt, y_sample,
            jnp.stack(kp_l), jnp.stack(vp_l), jnp.stack(cvp_l), jnp.stack(hp_l), jnp.stack(sp_l),
            jnp.stack(ks_l), jnp.stack(vs_l), jnp.stack(cvs_l), jnp.stack(hs_l), jnp.stack(ss_l))
```

```python
import functools

import numpy as np
import jax
import jax.numpy as jnp
from jax import lax
from jax.experimental import pallas as pl
from jax.experimental.pallas import tpu as pltpu

F32 = jnp.float32
BF16 = jnp.bfloat16

SB_DH = 64
SB_W = 512
LRU_W = 512
LRU_BLK = 64
CONV_TAPS = 4
LRU_SCALE = 8.0
GLA_H = 4
GLA_DK = 64
GLA_DV = 128
GLA_KW = GLA_H * GLA_DK
GLA_VW = GLA_H * GLA_DV
GLA_RANK = 16
GLA_TAU = 16.0
EPS = 1e-6
PAGE = 128

LANES = 128
SUBLANES = 8
VMEM_LIMIT = 56 * 1024 * 1024

SAMPLE_PAD_T = 8
GLA_SAMPLE_C = 32
GLA_IN_W = GLA_KW * 2 + GLA_VW * 2 + LANES
MIX_W = 3 * SB_W + 2 * LRU_W + GLA_IN_W


def _params(sem):
    return pltpu.CompilerParams(dimension_semantics=sem, vmem_limit_bytes=VMEM_LIMIT)


def _softplus(x):
    return jnp.maximum(x, 0.0) + jnp.log1p(jnp.exp(-jnp.abs(x)))


def _log_sigmoid(x):
    return jnp.minimum(x, 0.0) - jnp.log1p(jnp.exp(-jnp.abs(x)))


def _silu(x):
    return x * jax.nn.sigmoid(x)


def _gelu_tanh(x):
    return 0.5 * x * (1.0 + jnp.tanh(0.7978845608028654 * (x + 0.044715 * (x * x * x))))


def _split_bf16(x):
    hi = x.astype(BF16)
    lo = (x - hi.astype(F32)).astype(BF16)
    return hi, lo


def _mod_row(mod_ref, idx, per_token):
    if per_token:
        return mod_ref[idx]
    return mod_ref[idx:idx + 1, :]


def _norm_mod(x, g, shift, scale):
    xn = x * lax.rsqrt(jnp.mean(x * x, axis=-1, keepdims=True) + EPS) * g
    return xn * (1.0 + scale) + shift


def _mod_spec(per_token, tm, d, tok_per_seq):
    if per_token:
        return pl.BlockSpec((6, tm, d), lambda i, *_: (0, i, 0))
    blocks_per_seq = tok_per_seq // tm
    return pl.BlockSpec((None, 6, d), lambda i, *_: (i // blocks_per_seq, 0, 0))


def _ada_kernel(c_ref, w_ref, b_ref, o_ref):
    c = c_ref[...]
    s = _silu(c).astype(BF16)
    o_ref[...] = jnp.dot(s, w_ref[...].astype(BF16), preferred_element_type=F32) + b_ref[...]


def _ada(c_all, w_ada, b_ada):
    depth, d, e = w_ada.shape
    n = c_all.shape[0]
    tn = 1536 if e % 1536 == 0 else e
    return pl.pallas_call(
        _ada_kernel,
        grid=(depth, e // tn),
        in_specs=[
            pl.BlockSpec((n, d), lambda l, j: (0, 0)),
            pl.BlockSpec((None, d, tn), lambda l, j: (l, 0, j)),
            pl.BlockSpec((None, 1, tn), lambda l, j: (l, 0, j)),
        ],
        out_specs=pl.BlockSpec((None, n, tn), lambda l, j: (l, 0, j)),
        out_shape=jax.ShapeDtypeStruct((depth, n, e), F32),
        compiler_params=_params(("parallel", "parallel")),
    )(c_all, w_ada, b_ada.reshape(depth, 1, e))


def _inproj_kernel(x_ref, mod_ref, g_ref, w_ref,
                   q_ref, k_ref, v_ref, kb_ref, vb_ref, lru_ref, gla_ref, *, per_token):
    h = _norm_mod(x_ref[...], g_ref[...], _mod_row(mod_ref, 0, per_token),
                  _mod_row(mod_ref, 1, per_token)).astype(BF16)

    def seg(a, b):
        return jnp.dot(h, w_ref[:, a:b], preferred_element_type=F32)

    q_ref[...] = (seg(0, SB_W) * (SB_DH ** -0.5)).astype(BF16)
    k = seg(SB_W, 2 * SB_W)
    k_ref[...] = k
    kb_ref[...] = k.astype(BF16)
    v = seg(2 * SB_W, 3 * SB_W)
    v_ref[...] = v
    vb_ref[...] = v.astype(BF16)
    o = 3 * SB_W
    lru_ref[...] = seg(o, o + 2 * LRU_W)
    o += 2 * LRU_W
    gla_ref[...] = seg(o, o + GLA_IN_W)


def _inproj(x, mod, g, w_mix, *, per_token, tok_per_seq, tm):
    n, d = x.shape
    row = lambda w: pl.BlockSpec((tm, w), lambda i: (i, 0))
    outs = [(SB_W, BF16), (SB_W, F32), (SB_W, F32), (SB_W, BF16), (SB_W, BF16),
            (2 * LRU_W, F32), (GLA_IN_W, F32)]
    return pl.pallas_call(
        functools.partial(_inproj_kernel, per_token=per_token),
        grid=(n // tm,),
        in_specs=[
            row(d),
            _mod_spec(per_token, tm, d, tok_per_seq),
            pl.BlockSpec((1, d), lambda i: (0, 0)),
            pl.BlockSpec((d, MIX_W), lambda i: (0, 0)),
        ],
        out_specs=[row(w) for w, _ in outs],
        out_shape=[jax.ShapeDtypeStruct((n, w), dt) for w, dt in outs],
        compiler_params=_params(("parallel",)),
    )(x, mod, g, w_mix)


def _sb_tile(qh, kblk, vblk, bias, m2, carry, acc, valid, feature_major=False):
    k_dim = 0 if feature_major else 1
    z = lax.dot_general(qh, kblk, (((1,), (k_dim,)), ((), ())), preferred_element_type=F32) + bias
    t = jnp.log1p(jnp.exp(-jnp.abs(z)))
    log_beta = jnp.minimum(z, 0.0) - t
    log_keep = -jnp.maximum(z, 0.0) - t
    if valid is not None:
        log_keep = jnp.where(valid, log_keep, 0.0)
    hi, lo = _split_bf16(log_keep)
    sfx = jnp.dot(jnp.concatenate([hi, lo], axis=1), m2, preferred_element_type=F32)
    a = jnp.exp(log_beta + sfx + carry)
    if valid is not None:
        a = jnp.where(valid, a, 0.0)
    carry = carry + jnp.sum(log_keep, axis=1, keepdims=True)
    acc = acc + lax.dot_general(a.astype(BF16), vblk, (((1,), (1 - k_dim,)), ((), ())),
                                preferred_element_type=F32)
    return carry, acc


def _sb_prompt_kernel(bias_ref, q_ref, k_ref, v_ref, m2_ref, o_ref, *, tq):
    hp = pl.program_id(1)
    qi = pl.program_id(2)
    q = q_ref[...].astype(F32)
    m2 = m2_ref[...]
    lane = lax.broadcasted_iota(jnp.int32, (tq, LANES), 1)
    row = lax.broadcasted_iota(jnp.int32, (tq, tq), 0)
    col = lax.broadcasted_iota(jnp.int32, (tq, tq), 1)
    diag_valid = col < row
    accs = []
    for hh in range(2):
        own = lane < SB_DH if hh == 0 else lane >= SB_DH
        qh = jnp.where(own, q, 0.0).astype(BF16)
        bias = bias_ref[2 * hp + hh]

        def blk(j):
            start = pl.multiple_of(j * tq, tq)
            return k_ref[pl.ds(start, tq), :], v_ref[pl.ds(start, tq), :]

        kd, vd = blk(qi)
        carry, acc = _sb_tile(qh, kd, vd, bias, m2, jnp.zeros((tq, 1), F32),
                              jnp.zeros((tq, LANES), F32), diag_valid)

        def body(i, ca, qh=qh, bias=bias):
            kb, vb = blk(qi - 1 - i)
            return _sb_tile(qh, kb, vb, bias, m2, ca[0], ca[1], None)

        carry, acc = lax.fori_loop(0, qi, body, (carry, acc))
        accs.append(acc)
    o_ref[...] = jnp.where(lane < SB_DH, accs[0], accs[1]).astype(o_ref.dtype)


def _suffix_matrix(t):
    m = (np.arange(t)[:, None] > np.arange(t)[None, :]).astype(np.float32)
    return jnp.asarray(np.concatenate([m, m], axis=0), dtype=BF16)


def _sb_prompt(qb, kb, vb, bias, *, batch, seq, tq):
    n = qb.shape[0]
    nq = seq // tq
    return pl.pallas_call(
        functools.partial(_sb_prompt_kernel, tq=tq),
        grid=(batch, SB_W // LANES, nq),
        in_specs=[
            pl.BlockSpec(memory_space=pltpu.SMEM),
            pl.BlockSpec((tq, LANES), lambda b, h, i: (b * nq + i, h)),
            pl.BlockSpec((seq, LANES), lambda b, h, i: (b, h)),
            pl.BlockSpec((seq, LANES), lambda b, h, i: (b, h)),
            pl.BlockSpec((2 * tq, tq), lambda b, h, i: (0, 0)),
        ],
        out_specs=pl.BlockSpec((tq, LANES), lambda b, h, i: (b * nq + i, h)),
        out_shape=jax.ShapeDtypeStruct((n, SB_W), BF16),
        compiler_params=_params(("parallel", "parallel", "arbitrary")),
    )(bias, qb, kb, vb, _suffix_matrix(tq))


def _sb_sample_kernel(pt_ref, qbd_ref, kn_ref, vn_ref, bias_ref, m2_ref, *refs, group, n_new, n_heads):
    del pt_ref
    kp = refs[:group]
    vp = refs[group:2 * group]
    o_ref = refs[2 * group]
    carry_ref, acc_ref = refs[2 * group + 1:]
    p = pl.program_id(1)
    qbd = qbd_ref[...]
    bias = bias_ref[...]
    m2 = m2_ref[...]
    rows = n_new * n_heads

    @pl.when(p == 0)
    def _():
        pad = jnp.zeros((PAGE - SAMPLE_PAD_T, SB_W), F32)
        kn = jnp.concatenate([kn_ref[...], pad], axis=0).astype(BF16)
        vn = jnp.concatenate([vn_ref[...], pad], axis=0).astype(BF16)
        row = lax.broadcasted_iota(jnp.int32, (rows, PAGE), 0)
        col = lax.broadcasted_iota(jnp.int32, (rows, PAGE), 1)
        valid = col < row // n_heads
        carry, acc = _sb_tile(qbd, kn, vn, bias, m2, jnp.zeros((rows, 1), F32),
                              jnp.zeros((rows, SB_W), F32), valid)
        carry_ref[...] = carry
        acc_ref[...] = acc

    carry = carry_ref[...]
    acc = acc_ref[...]
    for i in range(group):
        carry, acc = _sb_tile(qbd, kp[i][...].astype(BF16), vp[i][...].astype(BF16),
                              bias, m2, carry, acc, None, feature_major=True)
    carry_ref[...] = carry
    acc_ref[...] = acc

    @pl.when(p == pl.num_programs(1) - 1)
    def _():
        row = lax.broadcasted_iota(jnp.int32, (rows, SB_W), 0)
        col = lax.broadcasted_iota(jnp.int32, (rows, SB_W), 1)
        own = jnp.where(col // SB_DH == row % n_heads, acc, 0.0)
        o_ref[...] = jnp.zeros(o_ref.shape, o_ref.dtype)
        for qq in range(n_new):
            o_ref[qq:qq + 1, :] = jnp.sum(own[qq * n_heads:(qq + 1) * n_heads, :], axis=0, keepdims=True)


def _sb_sample(qbd, k_new, v_new, bias_rows, cache_k, cache_v, page_table, layer, *, n_new, group):
    bs, n_pages = page_table.shape
    n_heads = SB_W // SB_DH
    rows = n_new * n_heads
    steps = n_pages // group

    def page_spec(i):
        def index(b, p, pt):
            return (layer, pt[b, n_pages - 1 - (p * group + i)], 0, 0)
        return pl.BlockSpec((None, None, SB_W, PAGE), index)

    grid_spec = pltpu.PrefetchScalarGridSpec(
        num_scalar_prefetch=1,
        grid=(bs, steps),
        in_specs=[
            pl.BlockSpec((None, rows, SB_W), lambda b, p, pt: (b, 0, 0)),
            pl.BlockSpec((SAMPLE_PAD_T, SB_W), lambda b, p, pt: (b, 0)),
            pl.BlockSpec((SAMPLE_PAD_T, SB_W), lambda b, p, pt: (b, 0)),
            pl.BlockSpec((rows, PAGE), lambda b, p, pt: (0, 0)),
            pl.BlockSpec((2 * PAGE, PAGE), lambda b, p, pt: (0, 0)),
        ] + [page_spec(i) for i in range(group)] * 2,
        out_specs=pl.BlockSpec((SAMPLE_PAD_T, SB_W), lambda b, p, pt: (b, 0)),
        scratch_shapes=[pltpu.VMEM((rows, 1), F32), pltpu.VMEM((rows, SB_W), F32)],
    )
    return pl.pallas_call(
        functools.partial(_sb_sample_kernel, group=group, n_new=n_new, n_heads=n_heads),
        grid_spec=grid_spec,
        out_shape=jax.ShapeDtypeStruct((bs * SAMPLE_PAD_T, SB_W), F32),
        compiler_params=_params(("parallel", "arbitrary")),
    )(page_table, qbd, k_new, v_new, bias_rows, _suffix_matrix(PAGE),
      *([cache_k] * group), *([cache_v] * group))


def _lru_kernel(x_ref, conv0_ref, h0_ref, cw_ref, cb_ref, wg_ref, ba_ref, bx_ref, lam_ref,
                y_ref, conv_ref, hl_ref, xpad, a_s, b_s, hs_s, hcar, *, tc, n_valid):
    ci = pl.program_id(1)
    head = SUBLANES
    tail = CONV_TAPS - 1

    @pl.when(ci == 0)
    def _():
        xpad[head - tail:head, :] = conv0_ref[...]
        hcar[...] = h0_ref[...]

    xpad[head:head + tc, :] = x_ref[:, 0:LRU_W]
    xc = cb_ref[...] + xpad[head - tail:head - tail + tc, :] * cw_ref[0:1, :]
    for w in range(1, CONV_TAPS):
        xc = xc + xpad[head - tail + w:head - tail + w + tc, :] * cw_ref[w:w + 1, :]
    xcb = xc.astype(BF16)
    neg_sp = -LRU_SCALE * _softplus(-lam_ref[...])
    for j in range(LRU_W // LANES):
        sl = slice(j * LANES, (j + 1) * LANES)
        g = jnp.dot(xcb[:, sl], wg_ref[j], preferred_element_type=F32)
        r = jax.nn.sigmoid(g[:, :LANES] + ba_ref[:, sl])
        gate_in = jax.nn.sigmoid(g[:, LANES:] + bx_ref[:, sl])
        log_a = neg_sp[:, sl] * r
        a = jnp.exp(log_a)
        mult = jnp.sqrt(-jnp.tanh(log_a) * (a * a + 1.0))
        a_s[:, sl] = a
        b_s[:, sl] = mult * (gate_in * xc[:, sl])

    rowi = lax.broadcasted_iota(jnp.int32, (SUBLANES, LRU_W), 0)

    def tile_scan(g, h):
        r0 = pl.multiple_of(g * SUBLANES, SUBLANES)
        a = a_s[pl.ds(r0, SUBLANES), :]
        b = b_s[pl.ds(r0, SUBLANES), :]
        for d in (1, 2, 4):
            keep = rowi >= d
            b = jnp.where(keep, a * pltpu.roll(b, d, axis=0) + b, b)
            a = jnp.where(keep, a * pltpu.roll(a, d, axis=0), a)
        hs = a * h + b
        hs_s[pl.ds(r0, SUBLANES), :] = hs
        return hs[SUBLANES - 1:SUBLANES, :]

    hcar[...] = lax.fori_loop(0, tc // SUBLANES, tile_scan, hcar[...])
    y_ref[...] = (hs_s[...] * _gelu_tanh(x_ref[:, LRU_W:2 * LRU_W])).astype(y_ref.dtype)

    new_tail = xpad[head + n_valid - tail:head + n_valid, :]
    conv_ref[...] = new_tail
    hl_ref[...] = hs_s[n_valid - 1:n_valid, :]
    xpad[head - tail:head, :] = new_tail


def _lru(lru_in, conv0, h0, cw, cb, wg, ba, bx, lam, *, batch, tok_per_seq, tc, n_valid, out_dtype):
    n = lru_in.shape[0]
    nc = tok_per_seq // tc
    vec = pl.BlockSpec((1, LRU_W), lambda b, c: (0, 0))
    tail = CONV_TAPS - 1
    return pl.pallas_call(
        functools.partial(_lru_kernel, tc=tc, n_valid=n_valid),
        grid=(batch, nc),
        in_specs=[
            pl.BlockSpec((tc, 2 * LRU_W), lambda b, c: (b * nc + c, 0)),
            pl.BlockSpec((None, tail, LRU_W), lambda b, c: (b, 0, 0)),
            pl.BlockSpec((None, 1, LRU_W), lambda b, c: (b, 0, 0)),
            pl.BlockSpec((CONV_TAPS, LRU_W), lambda b, c: (0, 0)),
            vec,
            pl.BlockSpec((LRU_W // LANES, LANES, 2 * LANES), lambda b, c: (0, 0, 0)),
            vec, vec, vec,
        ],
        out_specs=[
            pl.BlockSpec((tc, LRU_W), lambda b, c: (b * nc + c, 0)),
            pl.BlockSpec((None, tail, LRU_W), lambda b, c: (b, 0, 0)),
            pl.BlockSpec((None, 1, LRU_W), lambda b, c: (b, 0, 0)),
        ],
        out_shape=[
            jax.ShapeDtypeStruct((n, LRU_W), out_dtype),
            jax.ShapeDtypeStruct((batch, tail, LRU_W), F32),
            jax.ShapeDtypeStruct((batch, 1, LRU_W), F32),
        ],
        scratch_shapes=[
            pltpu.VMEM((SUBLANES + tc, LRU_W), F32),
            pltpu.VMEM((tc, LRU_W), F32),
            pltpu.VMEM((tc, LRU_W), F32),
            pltpu.VMEM((tc, LRU_W), F32),
            pltpu.VMEM((1, LRU_W), F32),
        ],
        compiler_params=_params(("parallel", "arbitrary")),
    )(lru_in, conv0, h0, cw, cb, wg, ba, bx, lam)


def _gla_kernel(x_ref, s0_ref, wal_ref, bal_ref, gn_ref, tri_ref, y_ref, sout_ref, st_ref, *, c, n_valid):
    ci = pl.program_id(1)

    @pl.when(ci == 0)
    def _():
        st_ref[...] = s0_ref[...].T

    x = x_ref[...]
    q = x[:, 0:GLA_KW] * (GLA_DK ** -0.5)
    k = x[:, GLA_KW:2 * GLA_KW]
    v = x[:, 2 * GLA_KW:2 * GLA_KW + GLA_VW]
    go = x[:, 2 * GLA_KW + GLA_VW:2 * GLA_KW + 2 * GLA_VW]
    glr = x[:, 2 * GLA_KW + 2 * GLA_VW:]
    xl = jnp.dot(glr.astype(BF16), wal_ref[...], preferred_element_type=F32) + bal_ref[...]
    la = _log_sigmoid(xl) * (1.0 / GLA_TAU)
    if n_valid < c:
        rowi = lax.broadcasted_iota(jnp.int32, (c, GLA_KW), 0)
        la = jnp.where(rowi < n_valid, la, 0.0)
        k = jnp.where(rowi < n_valid, k, 0.0)
    hi, lo = _split_bf16(la)
    tri = tri_ref[...]
    g = jnp.dot(tri, hi, preferred_element_type=F32) + jnp.dot(tri, lo, preferred_element_type=F32)
    g_last = g[c - 1:c, :]
    g_mid = g[c // 2:c // 2 + 1, :]
    q_in = q * jnp.exp(g)
    q_rel = (q * jnp.exp(g - g_mid)).astype(BF16)
    k_rel = k * jnp.exp(g_mid - g)
    k_dec = k * jnp.exp(g_last - g)

    lane_k = lax.broadcasted_iota(jnp.int32, (1, GLA_KW), 1) // GLA_DK
    kbd_rel = jnp.concatenate([jnp.where(lane_k == h, k_rel, 0.0) for h in range(GLA_H)], axis=0).astype(BF16)
    kbd_dec = jnp.concatenate([jnp.where(lane_k == h, k_dec, 0.0) for h in range(GLA_H)], axis=0).astype(BF16)
    att = lax.dot_general(q_rel, kbd_rel, (((1,), (1,)), ((), ())), preferred_element_type=F32)
    row = lax.broadcasted_iota(jnp.int32, (c, GLA_H * c), 0)
    col = lax.broadcasted_iota(jnp.int32, (c, GLA_H * c), 1)
    att = jnp.where(col % c <= row, att, 0.0)
    col_h = col // c

    v_stack = jnp.concatenate([v[:, h * GLA_DV:(h + 1) * GLA_DV] for h in range(GLA_H)], axis=0)
    v_t = v_stack.T
    st = st_ref[...]
    w_t = jnp.concatenate([st, v_t], axis=1).astype(BF16)
    lhs = jnp.concatenate(
        [jnp.concatenate([jnp.where(lane_k == h, q_in, 0.0), jnp.where(col_h == h, att, 0.0)], axis=1)
         for h in range(GLA_H)], axis=0).astype(BF16)
    res = lax.dot_general(lhs, w_t, (((1,), (1,)), ((), ())), preferred_element_type=F32)
    st_new = st * jnp.exp(g_last) + jnp.dot(v_t.astype(BF16), kbd_dec, preferred_element_type=F32)
    st_ref[...] = st_new

    outs = []
    for h in range(GLA_H):
        oh = res[h * c:(h + 1) * c, :]
        oh = oh * lax.rsqrt(jnp.mean(oh * oh, axis=-1, keepdims=True) + EPS)
        outs.append(oh * gn_ref[:, h * GLA_DV:(h + 1) * GLA_DV])
    o = jnp.concatenate(outs, axis=1)
    y_ref[...] = (o * _silu(go)).astype(y_ref.dtype)

    @pl.when(ci == pl.num_programs(1) - 1)
    def _():
        sout_ref[...] = st_new.T


def _gla(gla_in, s0, wal, bal, gn, *, batch, tok_per_seq, c, n_valid, out_dtype):
    n = gla_in.shape[0]
    nc = tok_per_seq // c
    tri = jnp.asarray((np.arange(c)[:, None] >= np.arange(c)[None, :]).astype(np.float32), dtype=BF16)
    state = pl.BlockSpec((None, GLA_KW, GLA_DV), lambda b, i: (b, 0, 0))
    return pl.pallas_call(
        functools.partial(_gla_kernel, c=c, n_valid=n_valid),
        grid=(batch, nc),
        in_specs=[
            pl.BlockSpec((c, GLA_IN_W), lambda b, i: (b * nc + i, 0)),
            state,
            pl.BlockSpec((LANES, GLA_KW), lambda b, i: (0, 0)),
            pl.BlockSpec((1, GLA_KW), lambda b, i: (0, 0)),
            pl.BlockSpec((1, GLA_VW), lambda b, i: (0, 0)),
            pl.BlockSpec((c, c), lambda b, i: (0, 0)),
        ],
        out_specs=[pl.BlockSpec((c, GLA_VW), lambda b, i: (b * nc + i, 0)), state],
        out_shape=[
            jax.ShapeDtypeStruct((n, GLA_VW), out_dtype),
            jax.ShapeDtypeStruct((batch, GLA_KW, GLA_DV), F32),
        ],
        scratch_shapes=[pltpu.VMEM((GLA_DV, GLA_KW), F32)],
        compiler_params=_params(("parallel", "arbitrary")),
    )(gla_in, s0, wal, bal, gn, tri)


def _merge_kernel(x_ref, mod_ref, g_ref, ysb_ref, ylru_ref, ygla_ref, wg_ref, wbr_ref, wout_ref,
                  o_ref, *, per_token):
    x = x_ref[...]
    d = x.shape[1]
    h = _norm_mod(x, g_ref[...], _mod_row(mod_ref, 0, per_token),
                  _mod_row(mod_ref, 1, per_token)).astype(BF16)
    merged = None
    for b, y_ref in enumerate((ysb_ref, ylru_ref, ygla_ref)):
        gate = jax.nn.sigmoid(jnp.dot(h, wg_ref[:, b * d:(b + 1) * d], preferred_element_type=F32))
        br = jnp.dot(y_ref[...].astype(BF16), wbr_ref[b], preferred_element_type=F32)
        merged = gate * br if merged is None else merged + gate * br
    out = jnp.dot(merged.astype(BF16), wout_ref[...], preferred_element_type=F32)
    o_ref[...] = x + _mod_row(mod_ref, 2, per_token) * out


def _merge(x, mod, g, ysb, ylru, ygla, wg, wbr, wout, *, per_token, tok_per_seq, tm):
    n, d = x.shape
    row = lambda w: pl.BlockSpec((tm, w), lambda i: (i, 0))
    return pl.pallas_call(
        functools.partial(_merge_kernel, per_token=per_token),
        grid=(n // tm,),
        in_specs=[
            row(d),
            _mod_spec(per_token, tm, d, tok_per_seq),
            pl.BlockSpec((1, d), lambda i: (0, 0)),
            row(SB_W), row(LRU_W), row(GLA_VW),
            pl.BlockSpec((d, 3 * d), lambda i: (0, 0)),
            pl.BlockSpec((3, SB_W, d), lambda i: (0, 0, 0)),
            pl.BlockSpec((d, d), lambda i: (0, 0)),
        ],
        out_specs=row(d),
        out_shape=jax.ShapeDtypeStruct((n, d), F32),
        compiler_params=_params(("parallel",)),
    )(x, mod, g, ysb, ylru, ygla, wg, wbr, wout)


def _ffn_kernel(x_ref, mod_ref, g_ref, wgate_ref, wup_ref, wdown_ref, gfin_ref, o_ref, h_s, acc_s,
                *, per_token, final):
    f = pl.program_id(1)

    @pl.when(f == 0)
    def _():
        h_s[...] = _norm_mod(x_ref[...], g_ref[...], _mod_row(mod_ref, 3, per_token),
                             _mod_row(mod_ref, 4, per_token)).astype(BF16)
        acc_s[...] = jnp.zeros(acc_s.shape, F32)

    h = h_s[...]
    ff = _silu(jnp.dot(h, wgate_ref[...], preferred_element_type=F32)) * jnp.dot(
        h, wup_ref[...], preferred_element_type=F32)
    acc_s[...] += jnp.dot(ff.astype(BF16), wdown_ref[...], preferred_element_type=F32)

    @pl.when(f == pl.num_programs(1) - 1)
    def _():
        x2 = x_ref[...] + _mod_row(mod_ref, 5, per_token) * acc_s[...]
        if final:
            x2 = x2 * lax.rsqrt(jnp.mean(x2 * x2, axis=-1, keepdims=True) + EPS) * gfin_ref[...]
        o_ref[...] = x2


def _ffn(x, mod, g, wgate, wup, wdown, gfin, *, per_token, tok_per_seq, tm, final):
    n, d = x.shape
    dff = wgate.shape[1]
    tf = dff // 2 if (dff // 2) % LANES == 0 else dff
    mod_spec = _mod_spec(per_token, tm, d, tok_per_seq)
    return pl.pallas_call(
        functools.partial(_ffn_kernel, per_token=per_token, final=final),
        grid=(n // tm, dff // tf),
        in_specs=[
            pl.BlockSpec((tm, d), lambda i, f: (i, 0)),
            mod_spec,
            pl.BlockSpec((1, d), lambda i, f: (0, 0)),
            pl.BlockSpec((d, tf), lambda i, f: (0, f)),
            pl.BlockSpec((d, tf), lambda i, f: (0, f)),
            pl.BlockSpec((tf, d), lambda i, f: (f, 0)),
            pl.BlockSpec((1, d), lambda i, f: (0, 0)),
        ],
        out_specs=pl.BlockSpec((tm, d), lambda i, f: (i, 0)),
        out_shape=jax.ShapeDtypeStruct((n, d), F32),
        scratch_shapes=[pltpu.VMEM((tm, d), BF16), pltpu.VMEM((tm, d), F32)],
        compiler_params=_params(("parallel", "arbitrary")),
    )(x, mod, g, wgate, wup, wdown, gfin)


def _split_w_in(w_in_l):
    d = w_in_l.shape[0]
    sizes = (SB_W, SB_W, SB_W, LRU_W, LRU_W, GLA_KW, GLA_KW, GLA_VW, GLA_VW, GLA_RANK, d, d, d)
    offs = np.concatenate([[0], np.cumsum(sizes)])
    seg = lambda i: w_in_l[:, offs[i]:offs[i + 1]]
    glr = jnp.pad(seg(9), ((0, 0), (0, LANES - GLA_RANK)))
    w_mix = jnp.concatenate([seg(i) for i in range(9)] + [glr], axis=1).astype(BF16)
    w_gates = jnp.concatenate([seg(10), seg(11), seg(12)], axis=1).astype(BF16)
    return w_mix, w_gates


def _lru_gate_tiles(wa, wx):
    nb = wa.shape[0]
    z = jnp.zeros((LRU_BLK, LRU_BLK), F32)

    def pair(w, j):
        return jnp.concatenate([jnp.concatenate([w[2 * j], z], axis=1),
                                jnp.concatenate([z, w[2 * j + 1]], axis=1)], axis=0)

    tiles = [jnp.concatenate([pair(wa, j), pair(wx, j)], axis=1) for j in range(nb // 2)]
    return jnp.stack(tiles).astype(BF16)


def _pick_tile(n, pref):
    t = pref
    while n % t:
        t //= 2
    return t


def kernel(x_prompt, x_sample, c_prompt, c_sample, cache_k, cache_v, page_table, state_conv, state_lru, state_gla, w_ada, b_ada, norm_mix, w_in, sb_bias, conv_w, conv_b, lru_wa, lru_ba, lru_wx, lru_bx, lru_lambda, gla_w_alpha, gla_b_alpha, gla_norm, w_br_sb, w_br_lru, w_br_gla, w_out, norm_ffn, w_gate, w_up, w_down, norm_final):
    bp, seq, d = x_prompt.shape
    bs, n_new, _ = x_sample.shape
    depth = w_ada.shape[0]
    n_heads = SB_W // SB_DH
    n_pool = cache_k.shape[1]
    tsp = SAMPLE_PAD_T

    xp = x_prompt.reshape(bp * seq, d)
    xs = jnp.pad(x_sample, ((0, 0), (0, tsp - n_new), (0, 0))).reshape(bs * tsp, d)
    ck = jnp.transpose(cache_k, (0, 1, 3, 4, 2)).reshape(depth, n_pool, SB_W, PAGE)
    cv = jnp.transpose(cache_v, (0, 1, 3, 4, 2)).reshape(depth, n_pool, SB_W, PAGE)

    mod = _ada(jnp.concatenate([c_prompt, c_sample], axis=0), w_ada, b_ada)
    mod = mod.reshape(depth, bp + bs, 6, d)
    mod_p = mod[:, :bp]
    mod_s = jnp.transpose(jnp.repeat(mod[:, bp:], tsp, axis=1), (0, 2, 1, 3))

    tm_p = _pick_tile(seq, 512)
    tm_s = bs * tsp
    tq = _pick_tile(seq, 256)
    tc_p = _pick_tile(seq, 256)
    c_gla = _pick_tile(seq, 64)
    group = _pick_tile(page_table.shape[1], 8)

    zeros_conv = jnp.zeros((bp, CONV_TAPS - 1, LRU_W), F32)
    zeros_h = jnp.zeros((bp, 1, LRU_W), F32)
    zeros_s = jnp.zeros((bp, GLA_KW, GLA_DV), F32)
    row2 = lambda a: a.reshape(1, -1)

    outs = {k: [] for k in ("kp", "vp", "cvp", "hp", "sp", "ks", "vs", "cvs", "hs", "ss")}
    for l in range(depth):
        w_mix, w_gates = _split_w_in(w_in[l])
        wg_lru = _lru_gate_tiles(lru_wa[l], lru_wx[l])
        wal = jnp.pad(gla_w_alpha[l], ((0, LANES - GLA_RANK), (0, 0))).astype(BF16)
        wbr = jnp.stack([w_br_sb[l], w_br_lru[l], w_br_gla[l]]).astype(BF16)
        wout = w_out[l].astype(BF16)
        wgate, wup, wdown = w_gate[l].astype(BF16), w_up[l].astype(BF16), w_down[l].astype(BF16)
        lru_args = (conv_w[l], row2(conv_b[l]), wg_lru, row2(lru_ba[l]), row2(lru_bx[l]), row2(lru_lambda[l]))
        gla_args = (wal, row2(gla_b_alpha[l]), row2(gla_norm[l]))
        final = l == depth - 1

        qb, k, v, kb, vb, lru_in, gla_in = _inproj(
            xp, mod_p[l], row2(norm_mix[l]), w_mix, per_token=False, tok_per_seq=seq, tm=tm_p)
        y_sb = _sb_prompt(qb, kb, vb, sb_bias[l], batch=bp, seq=seq, tq=tq)
        y_lru, cv1, h1 = _lru(lru_in, zeros_conv, zeros_h, *lru_args, batch=bp, tok_per_seq=seq,
                              tc=tc_p, n_valid=tc_p, out_dtype=BF16)
        y_gla, s1 = _gla(gla_in, zeros_s, *gla_args, batch=bp, tok_per_seq=seq, c=c_gla,
                         n_valid=c_gla, out_dtype=BF16)
        xp = _merge(xp, mod_p[l], row2(norm_mix[l]), y_sb, y_lru, y_gla, w_gates, wbr, wout,
                    per_token=False, tok_per_seq=seq, tm=tm_p)
        xp = _ffn(xp, mod_p[l], row2(norm_ffn[l]), wgate, wup, wdown, row2(norm_final),
                  per_token=False, tok_per_seq=seq, tm=tm_p, final=final)
        outs["kp"].append(k.reshape(bp, seq, n_heads, SB_DH))
        outs["vp"].append(v.reshape(bp, seq, n_heads, SB_DH))
        outs["cvp"].append(cv1)
        outs["hp"].append(h1.reshape(bp, LRU_W))
        outs["sp"].append(s1.reshape(bp, GLA_H, GLA_DK, GLA_DV))

        qb, k, v, _, _, lru_in, gla_in = _inproj(
            xs, mod_s[l], row2(norm_mix[l]), w_mix, per_token=True, tok_per_seq=tsp, tm=tm_s)
        q4 = qb.reshape(bs, tsp, n_heads, SB_DH)[:, :n_new]
        eye = jnp.eye(n_heads, dtype=BF16)
        qbd = (q4[:, :, None, :, :] * eye[None, None, :, :, None]).reshape(bs, n_new * n_heads, SB_W)
        bias_rows = jnp.broadcast_to(jnp.tile(sb_bias[l], n_new)[:, None], (n_new * n_heads, PAGE))
        y_sb = _sb_sample(qbd, k, v, bias_rows, ck, cv, page_table, l, n_new=n_new, group=group)
        y_lru, cv2, h2 = _lru(lru_in, state_conv[l], state_lru[l].reshape(bs, 1, LRU_W), *lru_args,
                              batch=bs, tok_per_seq=tsp, tc=tsp, n_valid=n_new, out_dtype=F32)
        gla_pad = jnp.pad(gla_in.reshape(bs, tsp, GLA_IN_W),
                          ((0, 0), (0, GLA_SAMPLE_C - tsp), (0, 0))).reshape(bs * GLA_SAMPLE_C, GLA_IN_W)
        y_gla, s2 = _gla(gla_pad, state_gla[l].reshape(bs, GLA_KW, GLA_DV), *gla_args, batch=bs,
                         tok_per_seq=GLA_SAMPLE_C, c=GLA_SAMPLE_C, n_valid=n_new, out_dtype=F32)
        y_gla = y_gla.reshape(bs, GLA_SAMPLE_C, GLA_VW)[:, :tsp].reshape(bs * tsp, GLA_VW)
        xs = _merge(xs, mod_s[l], row2(norm_mix[l]), y_sb, y_lru, y_gla, w_gates, wbr, wout,
                    per_token=True, tok_per_seq=tsp, tm=tm_s)
        xs = _ffn(xs, mod_s[l], row2(norm_ffn[l]), wgate, wup, wdown, row2(norm_final),
                  per_token=True, tok_per_seq=tsp, tm=tm_s, final=final)
        outs["ks"].append(k.reshape(bs, tsp, n_heads, SB_DH)[:, :n_new])
        outs["vs"].append(v.reshape(bs, tsp, n_heads, SB_DH)[:, :n_new])
        outs["cvs"].append(cv2)
        outs["hs"].append(h2.reshape(bs, LRU_W))
        outs["ss"].append(s2.reshape(bs, GLA_H, GLA_DK, GLA_DV))

    y_prompt = xp.reshape(bp, seq, d)
    y_sample = xs.reshape(bs, tsp, d)[:, :n_new]
    st = lambda name: jnp.stack(outs[name])
    return (y_prompt, y_sample, st("kp"), st("vp"), st("cvp"), st("hp"), st("sp"),
            st("ks"), st("vs"), st("cvs"), st("hs"), st("ss"))
```

```python
import functools

import numpy as np
import jax
import jax.numpy as jnp
from jax import lax
from jax.experimental import pallas as pl
from jax.experimental.pallas import tpu as pltpu

F32 = jnp.float32
BF16 = jnp.bfloat16

SB_DH = 64
SB_W = 512
LRU_W = 512
LRU_BLK = 64
CONV_TAPS = 4
LRU_SCALE = 8.0
GLA_H = 4
GLA_DK = 64
GLA_DV = 128
GLA_KW = GLA_H * GLA_DK
GLA_VW = GLA_H * GLA_DV
GLA_RANK = 16
GLA_TAU = 16.0
EPS = 1e-6
PAGE = 128

LANES = 128
SUBLANES = 8
VMEM_LIMIT = 56 * 1024 * 1024

SAMPLE_PAD_T = 8
GLA_SAMPLE_C = 32
GLA_IN_W = GLA_KW * 2 + GLA_VW * 2 + LANES
MIX_W = 3 * SB_W + 2 * LRU_W + GLA_IN_W


def _params(sem):
    return pltpu.CompilerParams(dimension_semantics=sem, vmem_limit_bytes=VMEM_LIMIT)


def _softplus(x):
    return jnp.maximum(x, 0.0) + jnp.log1p(jnp.exp(-jnp.abs(x)))


def _log_sigmoid(x):
    return jnp.minimum(x, 0.0) - jnp.log1p(jnp.exp(-jnp.abs(x)))


def _silu(x):
    return x * jax.nn.sigmoid(x)


def _gelu_tanh(x):
    return 0.5 * x * (1.0 + jnp.tanh(0.7978845608028654 * (x + 0.044715 * (x * x * x))))


def _split_bf16(x):
    hi = x.astype(BF16)
    lo = (x - hi.astype(F32)).astype(BF16)
    return hi, lo


def _mod_row(mod_ref, idx, per_token):
    if per_token:
        return mod_ref[idx]
    return mod_ref[idx:idx + 1, :]


def _norm_mod(x, g, shift, scale):
    xn = x * lax.rsqrt(jnp.mean(x * x, axis=-1, keepdims=True) + EPS) * g
    return xn * (1.0 + scale) + shift


def _mod_spec(per_token, tm, d, tok_per_seq):
    if per_token:
        return pl.BlockSpec((6, tm, d), lambda i, *_: (0, i, 0))
    blocks_per_seq = tok_per_seq // tm
    return pl.BlockSpec((None, 6, d), lambda i, *_: (i // blocks_per_seq, 0, 0))


def _ada_kernel(c_ref, w_ref, b_ref, o_ref):
    c = c_ref[...]
    s = _silu(c).astype(BF16)
    o_ref[...] = jnp.dot(s, w_ref[...].astype(BF16), preferred_element_type=F32) + b_ref[...]


def _ada(c_all, w_ada, b_ada):
    depth, d, e = w_ada.shape
    n = c_all.shape[0]
    tn = 1536 if e % 1536 == 0 else e
    return pl.pallas_call(
        _ada_kernel,
        grid=(depth, e // tn),
        in_specs=[
            pl.BlockSpec((n, d), lambda l, j: (0, 0)),
            pl.BlockSpec((None, d, tn), lambda l, j: (l, 0, j)),
            pl.BlockSpec((None, 1, tn), lambda l, j: (l, 0, j)),
        ],
        out_specs=pl.BlockSpec((None, n, tn), lambda l, j: (l, 0, j)),
        out_shape=jax.ShapeDtypeStruct((depth, n, e), F32),
        compiler_params=_params(("parallel", "parallel")),
    )(c_all, w_ada, b_ada.reshape(depth, 1, e))


def _inproj_kernel(x_ref, mod_ref, g_ref, w_ref,
                   q_ref, k_ref, v_ref, kb_ref, vb_ref, lru_ref, gla_ref, *, per_token):
    h = _norm_mod(x_ref[...], g_ref[...], _mod_row(mod_ref, 0, per_token),
                  _mod_row(mod_ref, 1, per_token)).astype(BF16)

    def seg(a, b):
        return jnp.dot(h, w_ref[:, a:b], preferred_element_type=F32)

    q_ref[...] = (seg(0, SB_W) * (SB_DH ** -0.5)).astype(BF16)
    k = seg(SB_W, 2 * SB_W)
    k_ref[...] = k
    kb_ref[...] = k.astype(BF16)
    v = seg(2 * SB_W, 3 * SB_W)
    v_ref[...] = v
    vb_ref[...] = v.astype(BF16)
    o = 3 * SB_W
    lru_ref[...] = seg(o, o + 2 * LRU_W)
    o += 2 * LRU_W
    gla_ref[...] = seg(o, o + GLA_IN_W)


def _inproj(x, mod, g, w_mix, *, per_token, tok_per_seq, tm):
    n, d = x.shape
    row = lambda w: pl.BlockSpec((tm, w), lambda i: (i, 0))
    outs = [(SB_W, BF16), (SB_W, F32), (SB_W, F32), (SB_W, BF16), (SB_W, BF16),
            (2 * LRU_W, F32), (GLA_IN_W, F32)]
    return pl.pallas_call(
        functools.partial(_inproj_kernel, per_token=per_token),
        grid=(n // tm,),
        in_specs=[
            row(d),
            _mod_spec(per_token, tm, d, tok_per_seq),
            pl.BlockSpec((1, d), lambda i: (0, 0)),
            pl.BlockSpec((d, MIX_W), lambda i: (0, 0)),
        ],
        out_specs=[row(w) for w, _ in outs],
        out_shape=[jax.ShapeDtypeStruct((n, w), dt) for w, dt in outs],
        compiler_params=_params(("parallel",)),
    )(x, mod, g, w_mix)


def _sb_logs(z):
    t = jnp.log(1.0 + jnp.exp(-jnp.abs(z)))
    m = jnp.minimum(z, 0.0)
    return m - t, (m - z) - t


def _sb_tile(qh, kblk, vblk, bias, m2, carry, acc, valid):
    z = lax.dot_general(qh, kblk, (((1,), (1,)), ((), ())), preferred_element_type=F32) + bias
    log_beta, log_keep = _sb_logs(z)
    if valid is not None:
        log_keep = jnp.where(valid, log_keep, 0.0)
    hi, lo = _split_bf16(log_keep)
    sfx = jnp.dot(jnp.concatenate([hi, lo], axis=1), m2, preferred_element_type=F32)
    a = jnp.exp(log_beta + sfx + carry)
    if valid is not None:
        a = jnp.where(valid, a, 0.0)
    carry = carry + jnp.sum(log_keep, axis=1, keepdims=True)
    acc = acc + jnp.dot(a.astype(BF16), vblk, preferred_element_type=F32)
    return carry, acc


def _sb_tile_pair(qh, kblk, vblk, bias, m2, carry, acc):
    rows = qh.shape[0]
    tk = kblk.shape[0] // 2
    z = lax.dot_general(qh, kblk, (((1,), (1,)), ((), ())), preferred_element_type=F32) + bias
    log_beta, log_keep = _sb_logs(z)
    lk_low, lk_high = log_keep[:, :tk], log_keep[:, tk:]
    split = jnp.concatenate([jnp.concatenate(_split_bf16(lk_low), axis=1),
                             jnp.concatenate(_split_bf16(lk_high), axis=1)], axis=0)
    sfx = jnp.dot(split, m2, preferred_element_type=F32)
    carry_low = carry + jnp.sum(lk_high, axis=1, keepdims=True)
    a = jnp.concatenate([jnp.exp(log_beta[:, :tk] + sfx[:rows] + carry_low),
                         jnp.exp(log_beta[:, tk:] + sfx[rows:] + carry)], axis=1)
    acc = acc + jnp.dot(a.astype(BF16), vblk, preferred_element_type=F32)
    return carry_low + jnp.sum(lk_low, axis=1, keepdims=True), acc


def _sb_prompt_kernel(bias_ref, q_ref, k_ref, v_ref, m2_ref, o_ref, *, tq):
    hp = pl.program_id(1)
    qi = pl.program_id(2)
    q = q_ref[...].astype(F32)
    m2 = m2_ref[...]
    lane = lax.broadcasted_iota(jnp.int32, (tq, LANES), 1)
    q2 = jnp.concatenate([jnp.where(lane < SB_DH, q, 0.0), jnp.where(lane >= SB_DH, q, 0.0)],
                         axis=0).astype(BF16)
    row1 = lax.broadcasted_iota(jnp.int32, (2 * tq, 1), 0)
    bias = jnp.where(row1 < tq, bias_ref[2 * hp], bias_ref[2 * hp + 1])
    row = lax.broadcasted_iota(jnp.int32, (2 * tq, tq), 0)
    col = lax.broadcasted_iota(jnp.int32, (2 * tq, tq), 1)
    diag_valid = col < (row & (tq - 1))

    def blk(j):
        start = pl.multiple_of(j * tq, tq)
        return k_ref[pl.ds(start, tq), :], v_ref[pl.ds(start, tq), :]

    kd, vd = blk(qi)
    carry, acc = _sb_tile(q2, kd, vd, bias, m2, jnp.zeros((2 * tq, 1), F32),
                          jnp.zeros((2 * tq, LANES), F32), diag_valid)

    def body(i, ca):
        start = pl.multiple_of((qi - 2 - 2 * i) * tq, tq)
        kb, vb = k_ref[pl.ds(start, 2 * tq), :], v_ref[pl.ds(start, 2 * tq), :]
        return _sb_tile_pair(q2, kb, vb, bias, m2, ca[0], ca[1])

    carry, acc = lax.fori_loop(0, qi // 2, body, (carry, acc))

    def last(ca):
        kb, vb = blk(0)
        return _sb_tile(q2, kb, vb, bias, m2, ca[0], ca[1], None)

    carry, acc = lax.cond(qi % 2 == 1, last, lambda ca: ca, (carry, acc))
    o_ref[...] = jnp.where(lane < SB_DH, acc[:tq], acc[tq:]).astype(o_ref.dtype)


def _suffix_matrix(t):
    m = (np.arange(t)[:, None] > np.arange(t)[None, :]).astype(np.float32)
    return jnp.asarray(np.concatenate([m, m], axis=0), dtype=BF16)


def _sb_prompt(qb, kb, vb, bias, *, batch, seq, tq):
    n = qb.shape[0]
    nq = seq // tq
    return pl.pallas_call(
        functools.partial(_sb_prompt_kernel, tq=tq),
        grid=(batch, SB_W // LANES, nq),
        in_specs=[
            pl.BlockSpec(memory_space=pltpu.SMEM),
            pl.BlockSpec((tq, LANES), lambda b, h, i: (b * nq + i, h)),
            pl.BlockSpec((seq, LANES), lambda b, h, i: (b, h)),
            pl.BlockSpec((seq, LANES), lambda b, h, i: (b, h)),
            pl.BlockSpec((2 * tq, tq), lambda b, h, i: (0, 0)),
        ],
        out_specs=pl.BlockSpec((tq, LANES), lambda b, h, i: (b * nq + i, h)),
        out_shape=jax.ShapeDtypeStruct((n, SB_W), BF16),
        compiler_params=_params(("parallel", "parallel", "arbitrary")),
    )(bias, qb, kb, vb, _suffix_matrix(tq))


def _sb_sample_kernel(pt_ref, qbd_ref, kn_ref, vn_ref, bias_ref, m2_ref, *refs, group, n_new, n_heads):
    del pt_ref
    kp = refs[:group]
    vp = refs[group:2 * group]
    o_ref = refs[2 * group]
    carry_ref, acc_ref = refs[2 * group + 1:]
    p = pl.program_id(1)
    qbd = qbd_ref[...]
    bias = bias_ref[...]
    m2 = m2_ref[...]
    rows = n_new * n_heads
    width = 2 * PAGE
    n_pair = group // 2

    @pl.when(p == 0)
    def _():
        pad = jnp.zeros((width - SAMPLE_PAD_T, SB_W), F32)
        kn = jnp.concatenate([kn_ref[...], pad], axis=0).astype(BF16)
        vn = jnp.concatenate([vn_ref[...], pad], axis=0).astype(BF16)
        row = lax.broadcasted_iota(jnp.int32, (rows, width), 0)
        col = lax.broadcasted_iota(jnp.int32, (rows, width), 1)
        valid = col < row // n_heads
        carry, acc = _sb_tile(qbd, kn, vn, bias, m2, jnp.zeros((rows, 1), F32),
                              jnp.zeros((rows, SB_W), F32), valid)
        carry_ref[...] = carry
        acc_ref[...] = acc

    def pair(page_refs, j):
        low = page_refs[group - 1 - 2 * j][...]
        high = page_refs[group - 2 - 2 * j][...]
        return jnp.concatenate([low, high], axis=1).astype(BF16)

    logs = [_sb_logs(jnp.dot(qbd, pair(kp, j), preferred_element_type=F32) + bias) for j in range(n_pair)]
    split = [jnp.concatenate(_split_bf16(lk), axis=1) for _, lk in logs]
    sfx = jnp.dot(jnp.concatenate(split, axis=0), m2, preferred_element_type=F32)
    carry = carry_ref[...]
    carries = [None] * n_pair
    for j in reversed(range(n_pair)):
        carries[j] = carry
        carry = carry + jnp.sum(logs[j][1], axis=1, keepdims=True)
    acc = acc_ref[...]
    for j in range(n_pair):
        a = jnp.exp(logs[j][0] + sfx[j * rows:(j + 1) * rows] + carries[j]).astype(BF16)
        acc = acc + lax.dot_general(a, pair(vp, j), (((1,), (1,)), ((), ())), preferred_element_type=F32)
    carry_ref[...] = carry
    acc_ref[...] = acc

    @pl.when(p == pl.num_programs(1) - 1)
    def _():
        row = lax.broadcasted_iota(jnp.int32, (rows, SB_W), 0)
        col = lax.broadcasted_iota(jnp.int32, (rows, SB_W), 1)
        own = jnp.where(col // SB_DH == row % n_heads, acc, 0.0)
        o_ref[...] = jnp.zeros(o_ref.shape, o_ref.dtype)
        for qq in range(n_new):
            o_ref[qq:qq + 1, :] = jnp.sum(own[qq * n_heads:(qq + 1) * n_heads, :], axis=0, keepdims=True)


def _sb_sample(qbd, k_new, v_new, bias_rows, cache_k, cache_v, page_table, layer, *, n_new, group):
    bs, n_pages = page_table.shape
    n_heads = SB_W // SB_DH
    rows = n_new * n_heads
    steps = n_pages // group

    def page_spec(i):
        def index(b, p, pt):
            return (layer, pt[b, n_pages - 1 - (p * group + i)], 0, 0)
        return pl.BlockSpec((None, None, SB_W, PAGE), index)

    grid_spec = pltpu.PrefetchScalarGridSpec(
        num_scalar_prefetch=1,
        grid=(bs, steps),
        in_specs=[
            pl.BlockSpec((None, rows, SB_W), lambda b, p, pt: (b, 0, 0)),
            pl.BlockSpec((SAMPLE_PAD_T, SB_W), lambda b, p, pt: (b, 0)),
            pl.BlockSpec((SAMPLE_PAD_T, SB_W), lambda b, p, pt: (b, 0)),
            pl.BlockSpec((rows, 2 * PAGE), lambda b, p, pt: (0, 0)),
            pl.BlockSpec((4 * PAGE, 2 * PAGE), lambda b, p, pt: (0, 0)),
        ] + [page_spec(i) for i in range(group)] * 2,
        out_specs=pl.BlockSpec((SAMPLE_PAD_T, SB_W), lambda b, p, pt: (b, 0)),
        scratch_shapes=[pltpu.VMEM((rows, 1), F32), pltpu.VMEM((rows, SB_W), F32)],
    )
    return pl.pallas_call(
        functools.partial(_sb_sample_kernel, group=group, n_new=n_new, n_heads=n_heads),
        grid_spec=grid_spec,
        out_shape=jax.ShapeDtypeStruct((bs * SAMPLE_PAD_T, SB_W), F32),
        compiler_params=_params(("parallel", "arbitrary")),
    )(page_table, qbd, k_new, v_new, bias_rows, _suffix_matrix(2 * PAGE),
      *([cache_k] * group), *([cache_v] * group))


def _lru_kernel(x_ref, conv0_ref, h0_ref, cw_ref, cb_ref, wg_ref, ba_ref, bx_ref, lam_ref,
                y_ref, conv_ref, hl_ref, xpad, a_s, b_s, hs_s, hcar, *, tc, n_valid):
    ci = pl.program_id(1)
    head = SUBLANES
    tail = CONV_TAPS - 1

    @pl.when(ci == 0)
    def _():
        xpad[head - tail:head, :] = conv0_ref[...]
        hcar[...] = h0_ref[...]

    xpad[head:head + tc, :] = x_ref[:, 0:LRU_W]
    xc = cb_ref[...] + xpad[head - tail:head - tail + tc, :] * cw_ref[0:1, :]
    for w in range(1, CONV_TAPS):
        xc = xc + xpad[head - tail + w:head - tail + w + tc, :] * cw_ref[w:w + 1, :]
    xcb = xc.astype(BF16)
    neg_sp = -LRU_SCALE * _softplus(-lam_ref[...])
    for j in range(LRU_W // LANES):
        sl = slice(j * LANES, (j + 1) * LANES)
        g = jnp.dot(xcb[:, sl], wg_ref[j], preferred_element_type=F32)
        r = jax.nn.sigmoid(g[:, :LANES] + ba_ref[:, sl])
        gate_in = jax.nn.sigmoid(g[:, LANES:] + bx_ref[:, sl])
        log_a = neg_sp[:, sl] * r
        a = jnp.exp(log_a)
        mult = jnp.sqrt(-jnp.tanh(log_a) * (a * a + 1.0))
        a_s[:, sl] = a
        b_s[:, sl] = mult * (gate_in * xc[:, sl])

    rowi = lax.broadcasted_iota(jnp.int32, (SUBLANES, LRU_W), 0)

    def tile_scan(g, h):
        r0 = pl.multiple_of(g * SUBLANES, SUBLANES)
        a = a_s[pl.ds(r0, SUBLANES), :]
        b = b_s[pl.ds(r0, SUBLANES), :]
        for d in (1, 2, 4):
            keep = rowi >= d
            b = jnp.where(keep, a * pltpu.roll(b, d, axis=0) + b, b)
            a = jnp.where(keep, a * pltpu.roll(a, d, axis=0), a)
        hs = a * h + b
        hs_s[pl.ds(r0, SUBLANES), :] = hs
        return hs[SUBLANES - 1:SUBLANES, :]

    hcar[...] = lax.fori_loop(0, tc // SUBLANES, tile_scan, hcar[...])
    y_ref[...] = (hs_s[...] * _gelu_tanh(x_ref[:, LRU_W:2 * LRU_W])).astype(y_ref.dtype)

    new_tail = xpad[head + n_valid - tail:head + n_valid, :]
    conv_ref[...] = new_tail
    hl_ref[...] = hs_s[n_valid - 1:n_valid, :]
    xpad[head - tail:head, :] = new_tail


def _lru(lru_in, conv0, h0, cw, cb, wg, ba, bx, lam, *, batch, tok_per_seq, tc, n_valid, out_dtype):
    n = lru_in.shape[0]
    nc = tok_per_seq // tc
    vec = pl.BlockSpec((1, LRU_W), lambda b, c: (0, 0))
    tail = CONV_TAPS - 1
    return pl.pallas_call(
        functools.partial(_lru_kernel, tc=tc, n_valid=n_valid),
        grid=(batch, nc),
        in_specs=[
            pl.BlockSpec((tc, 2 * LRU_W), lambda b, c: (b * nc + c, 0)),
            pl.BlockSpec((None, tail, LRU_W), lambda b, c: (b, 0, 0)),
            pl.BlockSpec((None, 1, LRU_W), lambda b, c: (b, 0, 0)),
            pl.BlockSpec((CONV_TAPS, LRU_W), lambda b, c: (0, 0)),
            vec,
            pl.BlockSpec((LRU_W // LANES, LANES, 2 * LANES), lambda b, c: (0, 0, 0)),
            vec, vec, vec,
        ],
        out_specs=[
            pl.BlockSpec((tc, LRU_W), lambda b, c: (b * nc + c, 0)),
            pl.BlockSpec((None, tail, LRU_W), lambda b, c: (b, 0, 0)),
            pl.BlockSpec((None, 1, LRU_W), lambda b, c: (b, 0, 0)),
        ],
        out_shape=[
            jax.ShapeDtypeStruct((n, LRU_W), out_dtype),
            jax.ShapeDtypeStruct((batch, tail, LRU_W), F32),
            jax.ShapeDtypeStruct((batch, 1, LRU_W), F32),
        ],
        scratch_shapes=[
            pltpu.VMEM((SUBLANES + tc, LRU_W), F32),
            pltpu.VMEM((tc, LRU_W), F32),
            pltpu.VMEM((tc, LRU_W), F32),
            pltpu.VMEM((tc, LRU_W), F32),
            pltpu.VMEM((1, LRU_W), F32),
        ],
        compiler_params=_params(("parallel", "arbitrary")),
    )(lru_in, conv0, h0, cw, cb, wg, ba, bx, lam)


def _gla_kernel(x_ref, s0_ref, wal_ref, bal_ref, gn_ref, tri_ref, y_ref, sout_ref, st_ref, *, c, n_valid, nseq):
    ci = pl.program_id(1)
    seqs = range(nseq)
    nt = (((1,), (1,)), ((), ()))

    @pl.when(ci == 0)
    def _():
        for s in seqs:
            st_ref[s] = s0_ref[s].T

    xs = [x_ref[s] for s in seqs]
    o_v = 2 * GLA_KW
    o_go = o_v + GLA_VW
    o_lr = o_go + GLA_VW
    glr = jnp.concatenate([x[:, o_lr:] for x in xs], axis=0).astype(BF16)
    xl = jnp.dot(glr, wal_ref[...], preferred_element_type=F32) + bal_ref[...]
    la = _log_sigmoid(xl) * (1.0 / GLA_TAU)
    ks = [x[:, GLA_KW:o_v] for x in xs]
    las = [la[s * c:(s + 1) * c] for s in seqs]
    if n_valid < c:
        rowi = lax.broadcasted_iota(jnp.int32, (c, GLA_KW), 0)
        las = [jnp.where(rowi < n_valid, l, 0.0) for l in las]
        ks = [jnp.where(rowi < n_valid, k, 0.0) for k in ks]
    parts = [_split_bf16(l) for l in las]
    tri = tri_ref[...]
    g_all = (jnp.dot(tri, jnp.concatenate([p[0] for p in parts], axis=1), preferred_element_type=F32)
             + jnp.dot(tri, jnp.concatenate([p[1] for p in parts], axis=1), preferred_element_type=F32))

    lane_k = lax.broadcasted_iota(jnp.int32, (1, GLA_KW), 1) // GLA_DK
    row = lax.broadcasted_iota(jnp.int32, (c, GLA_H * c), 0)
    col = lax.broadcasted_iota(jnp.int32, (c, GLA_H * c), 1)
    causal = col % c <= row
    col_h = col // c

    def by_head(a):
        return jnp.concatenate([jnp.where(lane_k == h, a, 0.0) for h in range(GLA_H)], axis=0)

    q_ins, q_rels, kbd_rels, kbd_decs, decays = [], [], [], [], []
    for s in seqs:
        g = g_all[:, s * GLA_KW:(s + 1) * GLA_KW]
        g_last = g[c - 1:c, :]
        g_mid = g[c // 2:c // 2 + 1, :]
        q = xs[s][:, 0:GLA_KW] * (GLA_DK ** -0.5)
        q_ins.append(q * jnp.exp(g))
        q_rels.append((q * jnp.exp(g - g_mid)).astype(BF16))
        kbd_rels.append(by_head(ks[s] * jnp.exp(g_mid - g)).astype(BF16))
        kbd_decs.append(by_head(ks[s] * jnp.exp(g_last - g)).astype(BF16))
        decays.append(jnp.exp(g_last))

    atts = [lax.dot_general(q_rels[s], kbd_rels[s], nt, preferred_element_type=F32) for s in seqs]
    v_ts, w_ts, lhss = [], [], []
    for s in seqs:
        v = xs[s][:, o_v:o_go]
        v_t = jnp.concatenate([v[:, h * GLA_DV:(h + 1) * GLA_DV] for h in range(GLA_H)], axis=0).T
        v_ts.append(v_t)
        w_ts.append(jnp.concatenate([st_ref[s], v_t], axis=1).astype(BF16))
        att = jnp.where(causal, atts[s], 0.0)
        lhss.append(jnp.concatenate(
            [jnp.concatenate([jnp.where(lane_k == h, q_ins[s], 0.0), jnp.where(col_h == h, att, 0.0)], axis=1)
             for h in range(GLA_H)], axis=0).astype(BF16))
    ress = [lax.dot_general(lhss[s], w_ts[s], nt, preferred_element_type=F32) for s in seqs]
    upds = [jnp.dot(v_ts[s].astype(BF16), kbd_decs[s], preferred_element_type=F32) for s in seqs]

    for s in seqs:
        st_new = st_ref[s] * decays[s] + upds[s]
        st_ref[s] = st_new
        outs = []
        for h in range(GLA_H):
            oh = ress[s][h * c:(h + 1) * c, :]
            oh = oh * lax.rsqrt(jnp.mean(oh * oh, axis=-1, keepdims=True) + EPS)
            outs.append(oh * gn_ref[:, h * GLA_DV:(h + 1) * GLA_DV])
        go = xs[s][:, o_go:o_lr]
        y_ref[s] = (jnp.concatenate(outs, axis=1) * _silu(go)).astype(y_ref.dtype)

    @pl.when(ci == pl.num_programs(1) - 1)
    def _():
        for s in seqs:
            sout_ref[s] = st_ref[s].T


def _gla(gla_in, s0, wal, bal, gn, *, batch, tok_per_seq, c, n_valid, out_dtype):
    nc = tok_per_seq // c
    nseq = 2 if batch % 2 == 0 else 1
    tri = jnp.asarray((np.arange(c)[:, None] >= np.arange(c)[None, :]).astype(np.float32), dtype=BF16)
    state = pl.BlockSpec((nseq, GLA_KW, GLA_DV), lambda b, i: (b, 0, 0))
    y, s_out = pl.pallas_call(
        functools.partial(_gla_kernel, c=c, n_valid=n_valid, nseq=nseq),
        grid=(batch // nseq, nc),
        in_specs=[
            pl.BlockSpec((nseq, c, GLA_IN_W), lambda b, i: (b, i, 0)),
            state,
            pl.BlockSpec((LANES, GLA_KW), lambda b, i: (0, 0)),
            pl.BlockSpec((1, GLA_KW), lambda b, i: (0, 0)),
            pl.BlockSpec((1, GLA_VW), lambda b, i: (0, 0)),
            pl.BlockSpec((c, c), lambda b, i: (0, 0)),
        ],
        out_specs=[pl.BlockSpec((nseq, c, GLA_VW), lambda b, i: (b, i, 0)), state],
        out_shape=[
            jax.ShapeDtypeStruct((batch, tok_per_seq, GLA_VW), out_dtype),
            jax.ShapeDtypeStruct((batch, GLA_KW, GLA_DV), F32),
        ],
        scratch_shapes=[pltpu.VMEM((nseq, GLA_DV, GLA_KW), F32)],
        compiler_params=_params(("parallel", "arbitrary")),
    )(gla_in.reshape(batch, tok_per_seq, GLA_IN_W), s0, wal, bal, gn, tri)
    return y.reshape(batch * tok_per_seq, GLA_VW), s_out


def _merge_kernel(x_ref, mod_ref, g_ref, ysb_ref, ylru_ref, ygla_ref, wg_ref, wbr_ref, wout_ref,
                  o_ref, *, per_token):
    x = x_ref[...]
    d = x.shape[1]
    h = _norm_mod(x, g_ref[...], _mod_row(mod_ref, 0, per_token),
                  _mod_row(mod_ref, 1, per_token)).astype(BF16)
    merged = None
    for b, y_ref in enumerate((ysb_ref, ylru_ref, ygla_ref)):
        gate = jax.nn.sigmoid(jnp.dot(h, wg_ref[:, b * d:(b + 1) * d], preferred_element_type=F32))
        br = jnp.dot(y_ref[...].astype(BF16), wbr_ref[b], preferred_element_type=F32)
        merged = gate * br if merged is None else merged + gate * br
    out = jnp.dot(merged.astype(BF16), wout_ref[...], preferred_element_type=F32)
    o_ref[...] = x + _mod_row(mod_ref, 2, per_token) * out


def _merge(x, mod, g, ysb, ylru, ygla, wg, wbr, wout, *, per_token, tok_per_seq, tm):
    n, d = x.shape
    row = lambda w: pl.BlockSpec((tm, w), lambda i: (i, 0))
    return pl.pallas_call(
        functools.partial(_merge_kernel, per_token=per_token),
        grid=(n // tm,),
        in_specs=[
            row(d),
            _mod_spec(per_token, tm, d, tok_per_seq),
            pl.BlockSpec((1, d), lambda i: (0, 0)),
            row(SB_W), row(LRU_W), row(GLA_VW),
            pl.BlockSpec((d, 3 * d), lambda i: (0, 0)),
            pl.BlockSpec((3, SB_W, d), lambda i: (0, 0, 0)),
            pl.BlockSpec((d, d), lambda i: (0, 0)),
        ],
        out_specs=row(d),
        out_shape=jax.ShapeDtypeStruct((n, d), F32),
        compiler_params=_params(("parallel",)),
    )(x, mod, g, ysb, ylru, ygla, wg, wbr, wout)


def _ffn_kernel(x_ref, mod_ref, g_ref, wgate_ref, wup_ref, wdown_ref, gfin_ref, o_ref, h_s, acc_s,
                *, per_token, final):
    f = pl.program_id(1)

    @pl.when(f == 0)
    def _():
        h_s[...] = _norm_mod(x_ref[...], g_ref[...], _mod_row(mod_ref, 3, per_token),
                             _mod_row(mod_ref, 4, per_token)).astype(BF16)
        acc_s[...] = jnp.zeros(acc_s.shape, F32)

    h = h_s[...]
    ff = _silu(jnp.dot(h, wgate_ref[...], preferred_element_type=F32)) * jnp.dot(
        h, wup_ref[...], preferred_element_type=F32)
    acc_s[...] += jnp.dot(ff.astype(BF16), wdown_ref[...], preferred_element_type=F32)

    @pl.when(f == pl.num_programs(1) - 1)
    def _():
        x2 = x_ref[...] + _mod_row(mod_ref, 5, per_token) * acc_s[...]
        if final:
            x2 = x2 * lax.rsqrt(jnp.mean(x2 * x2, axis=-1, keepdims=True) + EPS) * gfin_ref[...]
        o_ref[...] = x2


def _ffn(x, mod, g, wgate, wup, wdown, gfin, *, per_token, tok_per_seq, tm, final):
    n, d = x.shape
    dff = wgate.shape[1]
    tf = dff // 2 if (dff // 2) % LANES == 0 else dff
    mod_spec = _mod_spec(per_token, tm, d, tok_per_seq)
    return pl.pallas_call(
        functools.partial(_ffn_kernel, per_token=per_token, final=final),
        grid=(n // tm, dff // tf),
        in_specs=[
            pl.BlockSpec((tm, d), lambda i, f: (i, 0)),
            mod_spec,
            pl.BlockSpec((1, d), lambda i, f: (0, 0)),
            pl.BlockSpec((d, tf), lambda i, f: (0, f)),
            pl.BlockSpec((d, tf), lambda i, f: (0, f)),
            pl.BlockSpec((tf, d), lambda i, f: (f, 0)),
            pl.BlockSpec((1, d), lambda i, f: (0, 0)),
        ],
        out_specs=pl.BlockSpec((tm, d), lambda i, f: (i, 0)),
        out_shape=jax.ShapeDtypeStruct((n, d), F32),
        scratch_shapes=[pltpu.VMEM((tm, d), BF16), pltpu.VMEM((tm, d), F32)],
        compiler_params=_params(("parallel", "arbitrary")),
    )(x, mod, g, wgate, wup, wdown, gfin)


def _split_w_in(w_in_l):
    d = w_in_l.shape[0]
    sizes = (SB_W, SB_W, SB_W, LRU_W, LRU_W, GLA_KW, GLA_KW, GLA_VW, GLA_VW, GLA_RANK, d, d, d)
    offs = np.concatenate([[0], np.cumsum(sizes)])
    seg = lambda i: w_in_l[:, offs[i]:offs[i + 1]]
    glr = jnp.pad(seg(9), ((0, 0), (0, LANES - GLA_RANK)))
    w_mix = jnp.concatenate([seg(i) for i in range(9)] + [glr], axis=1).astype(BF16)
    w_gates = jnp.concatenate([seg(10), seg(11), seg(12)], axis=1).astype(BF16)
    return w_mix, w_gates


def _lru_gate_tiles(wa, wx):
    nb = wa.shape[0]
    z = jnp.zeros((LRU_BLK, LRU_BLK), F32)

    def pair(w, j):
        return jnp.concatenate([jnp.concatenate([w[2 * j], z], axis=1),
                                jnp.concatenate([z, w[2 * j + 1]], axis=1)], axis=0)

    tiles = [jnp.concatenate([pair(wa, j), pair(wx, j)], axis=1) for j in range(nb // 2)]
    return jnp.stack(tiles).astype(BF16)


def _pick_tile(n, pref):
    t = pref
    while n % t:
        t //= 2
    return t


def kernel(x_prompt, x_sample, c_prompt, c_sample, cache_k, cache_v, page_table, state_conv, state_lru, state_gla, w_ada, b_ada, norm_mix, w_in, sb_bias, conv_w, conv_b, lru_wa, lru_ba, lru_wx, lru_bx, lru_lambda, gla_w_alpha, gla_b_alpha, gla_norm, w_br_sb, w_br_lru, w_br_gla, w_out, norm_ffn, w_gate, w_up, w_down, norm_final):
    bp, seq, d = x_prompt.shape
    bs, n_new, _ = x_sample.shape
    depth = w_ada.shape[0]
    n_heads = SB_W // SB_DH
    n_pool = cache_k.shape[1]
    tsp = SAMPLE_PAD_T

    xp = x_prompt.reshape(bp * seq, d)
    xs = jnp.pad(x_sample, ((0, 0), (0, tsp - n_new), (0, 0))).reshape(bs * tsp, d)
    ck = jnp.transpose(cache_k, (0, 1, 3, 4, 2)).reshape(depth, n_pool, SB_W, PAGE)
    cv = jnp.transpose(cache_v, (0, 1, 3, 4, 2)).reshape(depth, n_pool, SB_W, PAGE)

    mod = _ada(jnp.concatenate([c_prompt, c_sample], axis=0), w_ada, b_ada)
    mod = mod.reshape(depth, bp + bs, 6, d)
    mod_p = mod[:, :bp]
    mod_s = jnp.transpose(jnp.repeat(mod[:, bp:], tsp, axis=1), (0, 2, 1, 3))

    tm_p = _pick_tile(seq, 512)
    tm_s = bs * tsp
    tq = _pick_tile(seq, 256)
    tc_p = _pick_tile(seq, 256)
    c_gla = _pick_tile(seq, 64)
    group = _pick_tile(page_table.shape[1], 8)

    zeros_conv = jnp.zeros((bp, CONV_TAPS - 1, LRU_W), F32)
    zeros_h = jnp.zeros((bp, 1, LRU_W), F32)
    zeros_s = jnp.zeros((bp, GLA_KW, GLA_DV), F32)
    row2 = lambda a: a.reshape(1, -1)

    outs = {k: [] for k in ("kp", "vp", "cvp", "hp", "sp", "ks", "vs", "cvs", "hs", "ss")}
    for l in range(depth):
        w_mix, w_gates = _split_w_in(w_in[l])
        wg_lru = _lru_gate_tiles(lru_wa[l], lru_wx[l])
        wal = jnp.pad(gla_w_alpha[l], ((0, LANES - GLA_RANK), (0, 0))).astype(BF16)
        wbr = jnp.stack([w_br_sb[l], w_br_lru[l], w_br_gla[l]]).astype(BF16)
        wout = w_out[l].astype(BF16)
        wgate, wup, wdown = w_gate[l].astype(BF16), w_up[l].astype(BF16), w_down[l].astype(BF16)
        lru_args = (conv_w[l], row2(conv_b[l]), wg_lru, row2(lru_ba[l]), row2(lru_bx[l]), row2(lru_lambda[l]))
        gla_args = (wal, row2(gla_b_alpha[l]), row2(gla_norm[l]))
        final = l == depth - 1

        qb, k, v, kb, vb, lru_in, gla_in = _inproj(
            xp, mod_p[l], row2(norm_mix[l]), w_mix, per_token=False, tok_per_seq=seq, tm=tm_p)
        y_sb = _sb_prompt(qb, kb, vb, sb_bias[l], batch=bp, seq=seq, tq=tq)
        y_lru, cv1, h1 = _lru(lru_in, zeros_conv, zeros_h, *lru_args, batch=bp, tok_per_seq=seq,
                              tc=tc_p, n_valid=tc_p, out_dtype=BF16)
        y_gla, s1 = _gla(gla_in, zeros_s, *gla_args, batch=bp, tok_per_seq=seq, c=c_gla,
                         n_valid=c_gla, out_dtype=BF16)
        xp = _merge(xp, mod_p[l], row2(norm_mix[l]), y_sb, y_lru, y_gla, w_gates, wbr, wout,
                    per_token=False, tok_per_seq=seq, tm=tm_p)
        xp = _ffn(xp, mod_p[l], row2(norm_ffn[l]), wgate, wup, wdown, row2(norm_final),
                  per_token=False, tok_per_seq=seq, tm=tm_p, final=final)
        outs["kp"].append(k.reshape(bp, seq, n_heads, SB_DH))
        outs["vp"].append(v.reshape(bp, seq, n_heads, SB_DH))
        outs["cvp"].append(cv1)
        outs["hp"].append(h1.reshape(bp, LRU_W))
        outs["sp"].append(s1.reshape(bp, GLA_H, GLA_DK, GLA_DV))

        qb, k, v, _, _, lru_in, gla_in = _inproj(
            xs, mod_s[l], row2(norm_mix[l]), w_mix, per_token=True, tok_per_seq=tsp, tm=tm_s)
        q4 = qb.reshape(bs, tsp, n_heads, SB_DH)[:, :n_new]
        eye = jnp.eye(n_heads, dtype=BF16)
        qbd = (q4[:, :, None, :, :] * eye[None, None, :, :, None]).reshape(bs, n_new * n_heads, SB_W)
        bias_rows = jnp.broadcast_to(jnp.tile(sb_bias[l], n_new)[:, None], (n_new * n_heads, 2 * PAGE))
        y_sb = _sb_sample(qbd, k, v, bias_rows, ck, cv, page_table, l, n_new=n_new, group=group)
        y_lru, cv2, h2 = _lru(lru_in, state_conv[l], state_lru[l].reshape(bs, 1, LRU_W), *lru_args,
                              batch=bs, tok_per_seq=tsp, tc=tsp, n_valid=n_new, out_dtype=F32)
        gla_pad = jnp.pad(gla_in.reshape(bs, tsp, GLA_IN_W),
                          ((0, 0), (0, GLA_SAMPLE_C - tsp), (0, 0))).reshape(bs * GLA_SAMPLE_C, GLA_IN_W)
        y_gla, s2 = _gla(gla_pad, state_gla[l].reshape(bs, GLA_KW, GLA_DV), *gla_args, batch=bs,
                         tok_per_seq=GLA_SAMPLE_C, c=GLA_SAMPLE_C, n_valid=n_new, out_dtype=F32)
        y_gla = y_gla.reshape(bs, GLA_SAMPLE_C, GLA_VW)[:, :tsp].reshape(bs * tsp, GLA_VW)
        xs = _merge(xs, mod_s[l], row2(norm_mix[l]), y_sb, y_lru, y_gla, w_gates, wbr, wout,
                    per_token=True, tok_per_seq=tsp, tm=tm_s)
        xs = _ffn(xs, mod_s[l], row2(norm_ffn[l]), wgate, wup, wdown, row2(norm_final),
                  per_token=True, tok_per_seq=tsp, tm=tm_s, final=final)
        outs["ks"].append(k.reshape(bs, tsp, n_heads, SB_DH)[:, :n_new])
        outs["vs"].append(v.reshape(bs, tsp, n_heads, SB_DH)[:, :n_new])
        outs["cvs"].append(cv2)
        outs["hs"].append(h2.reshape(bs, LRU_W))
        outs["ss"].append(s2.reshape(bs, GLA_H, GLA_DK, GLA_DV))

    y_prompt = xp.reshape(bp, seq, d)
    y_sample = xs.reshape(bs, tsp, d)[:, :n_new]
    st = lambda name: jnp.stack(outs[name])
    return (y_prompt, y_sample, st("kp"), st("vp"), st("cvp"), st("hp"), st("sp"),
            st("ks"), st("vs"), st("cvs"), st("hs"), st("ss"))
```

```python
import functools

import numpy as np
import jax
import jax.numpy as jnp
from jax import lax
from jax.experimental import pallas as pl
from jax.experimental.pallas import tpu as pltpu

F32 = jnp.float32
BF16 = jnp.bfloat16

SB_DH = 64
SB_W = 512
LRU_W = 512
LRU_BLK = 64
CONV_TAPS = 4
LRU_SCALE = 8.0
GLA_H = 4
GLA_DK = 64
GLA_DV = 128
GLA_KW = GLA_H * GLA_DK
GLA_VW = GLA_H * GLA_DV
GLA_RANK = 16
GLA_TAU = 16.0
EPS = 1e-6
PAGE = 128

LANES = 128
SUBLANES = 8
VMEM_LIMIT = 56 * 1024 * 1024

SAMPLE_PAD_T = 8
GLA_SAMPLE_C = 32
GLA_IN_W = GLA_KW * 2 + GLA_VW * 2 + LANES
MIX_W = 3 * SB_W + 2 * LRU_W + GLA_IN_W


def _params(sem):
    return pltpu.CompilerParams(dimension_semantics=sem, vmem_limit_bytes=VMEM_LIMIT)


def _softplus(x):
    return jnp.maximum(x, 0.0) + jnp.log1p(jnp.exp(-jnp.abs(x)))


def _log_sigmoid(x):
    return jnp.minimum(x, 0.0) - jnp.log1p(jnp.exp(-jnp.abs(x)))


def _silu(x):
    return x * jax.nn.sigmoid(x)


def _gelu_tanh(x):
    return 0.5 * x * (1.0 + jnp.tanh(0.7978845608028654 * (x + 0.044715 * (x * x * x))))


def _split_bf16(x):
    hi = x.astype(BF16)
    lo = (x - hi.astype(F32)).astype(BF16)
    return hi, lo


def _mod_row(mod_ref, idx, per_token):
    if per_token:
        return mod_ref[idx]
    return mod_ref[idx:idx + 1, :]


def _norm_mod(x, g, shift, scale):
    xn = x * lax.rsqrt(jnp.mean(x * x, axis=-1, keepdims=True) + EPS) * g
    return xn * (1.0 + scale) + shift


def _mod_spec(per_token, tm, d, tok_per_seq):
    if per_token:
        return pl.BlockSpec((6, tm, d), lambda i, *_: (0, i, 0))
    blocks_per_seq = tok_per_seq // tm
    return pl.BlockSpec((None, 6, d), lambda i, *_: (i // blocks_per_seq, 0, 0))


def _ada_kernel(c_ref, w_ref, b_ref, o_ref):
    c = c_ref[...]
    s = _silu(c).astype(BF16)
    o_ref[...] = jnp.dot(s, w_ref[...].astype(BF16), preferred_element_type=F32) + b_ref[...]


def _ada(c_all, w_ada, b_ada):
    depth, d, e = w_ada.shape
    n = c_all.shape[0]
    tn = 1536 if e % 1536 == 0 else e
    return pl.pallas_call(
        _ada_kernel,
        grid=(depth, e // tn),
        in_specs=[
            pl.BlockSpec((n, d), lambda l, j: (0, 0)),
            pl.BlockSpec((None, d, tn), lambda l, j: (l, 0, j)),
            pl.BlockSpec((None, 1, tn), lambda l, j: (l, 0, j)),
        ],
        out_specs=pl.BlockSpec((None, n, tn), lambda l, j: (l, 0, j)),
        out_shape=jax.ShapeDtypeStruct((depth, n, e), F32),
        compiler_params=_params(("parallel", "parallel")),
    )(c_all, w_ada, b_ada.reshape(depth, 1, e))


def _inproj_kernel(x_ref, mod_ref, g_ref, w_ref, *rest, per_token, kv_slab):
    q_ref, k_ref, v_ref, kb_ref, vb_ref, lru_ref, gla_ref = rest[-7:]
    h = _norm_mod(x_ref[...], g_ref[...], _mod_row(mod_ref, 0, per_token),
                  _mod_row(mod_ref, 1, per_token)).astype(BF16)

    def seg(a, b):
        return jnp.dot(h, w_ref[:, a:b], preferred_element_type=F32)

    q_ref[...] = (seg(0, SB_W) * (SB_DH ** -0.5)).astype(BF16)
    k = seg(SB_W, 2 * SB_W)
    k_ref[...] = k.T if kv_slab else k
    kb_ref[...] = k.astype(BF16)
    v = seg(2 * SB_W, 3 * SB_W)
    v_ref[...] = v.T if kv_slab else v
    vb_ref[...] = v.astype(BF16)
    o = 3 * SB_W
    lru_ref[...] = seg(o, o + 2 * LRU_W)
    o += 2 * LRU_W
    gla_ref[...] = seg(o, o + GLA_IN_W)


def _inproj(x, mod, g, w_mix, *, per_token, tok_per_seq, tm, kv_slab=None):
    n, d = x.shape
    row = lambda w: pl.BlockSpec((tm, w), lambda i: (i, 0))
    outs = [(SB_W, BF16), (SB_W, F32), (SB_W, F32), (SB_W, BF16), (SB_W, BF16),
            (2 * LRU_W, F32), (GLA_IN_W, F32)]
    out_specs = [row(w) for w, _ in outs]
    out_shape = [jax.ShapeDtypeStruct((n, w), dt) for w, dt in outs]
    extra_in, extra_specs, aliases = [], [], {}
    if kv_slab is not None:
        layer, depth, k_slab, v_slab = kv_slab
        bps = tok_per_seq // tm
        slab_spec = pl.BlockSpec((None, None, SB_W, tm), lambda i: (layer, i // bps, 0, i % bps))
        slab_shape = jax.ShapeDtypeStruct((depth, n // tok_per_seq, SB_W, tok_per_seq), F32)
        out_specs[1] = out_specs[2] = slab_spec
        out_shape[1] = out_shape[2] = slab_shape
        if k_slab is not None:
            extra_in = [k_slab, v_slab]
            extra_specs = [pl.BlockSpec(memory_space=pl.ANY)] * 2
            aliases = {4: 1, 5: 2}
    return pl.pallas_call(
        functools.partial(_inproj_kernel, per_token=per_token, kv_slab=kv_slab is not None),
        grid=(n // tm,),
        in_specs=[
            row(d),
            _mod_spec(per_token, tm, d, tok_per_seq),
            pl.BlockSpec((1, d), lambda i: (0, 0)),
            pl.BlockSpec((d, MIX_W), lambda i: (0, 0)),
        ] + extra_specs,
        out_specs=out_specs,
        out_shape=out_shape,
        input_output_aliases=aliases,
        compiler_params=_params(("parallel",)),
    )(x, mod, g, w_mix, *extra_in)


def _sb_logs(z):
    t = jnp.log(1.0 + jnp.exp(-jnp.abs(z)))
    m = jnp.minimum(z, 0.0)
    return m - t, (m - z) - t


def _suffix_lhs(log_keep, m2):
    if m2.shape[0] == log_keep.shape[1]:
        return log_keep.astype(BF16)
    return jnp.concatenate(_split_bf16(log_keep), axis=1)


def _sb_tile(qh, kblk, vblk, bias, m2, carry, acc, valid):
    z = lax.dot_general(qh, kblk, (((1,), (1,)), ((), ())), preferred_element_type=F32) + bias
    log_beta, log_keep = _sb_logs(z)
    if valid is not None:
        log_keep = jnp.where(valid, log_keep, 0.0)
    sfx = jnp.dot(_suffix_lhs(log_keep, m2), m2, preferred_element_type=F32)
    a = jnp.exp(log_beta + sfx + carry)
    if valid is not None:
        a = jnp.where(valid, a, 0.0)
    carry = carry + jnp.sum(log_keep, axis=1, keepdims=True)
    acc = acc + jnp.dot(a.astype(BF16), vblk, preferred_element_type=F32)
    return carry, acc


def _sb_tile_pair(qh, kblk, vblk, bias, m2, carry, acc):
    rows = qh.shape[0]
    tk = kblk.shape[0] // 2
    z = lax.dot_general(qh, kblk, (((1,), (1,)), ((), ())), preferred_element_type=F32) + bias
    log_beta, log_keep = _sb_logs(z)
    lk_low, lk_high = log_keep[:, :tk], log_keep[:, tk:]
    sfx = jnp.dot(jnp.concatenate([_suffix_lhs(lk_low, m2), _suffix_lhs(lk_high, m2)], axis=0), m2,
                  preferred_element_type=F32)
    carry_low = carry + jnp.sum(lk_high, axis=1, keepdims=True)
    a = jnp.concatenate([jnp.exp(log_beta[:, :tk] + sfx[:rows] + carry_low),
                         jnp.exp(log_beta[:, tk:] + sfx[rows:] + carry)], axis=1)
    acc = acc + jnp.dot(a.astype(BF16), vblk, preferred_element_type=F32)
    return carry_low + jnp.sum(lk_low, axis=1, keepdims=True), acc


def _sb_prompt_kernel(bias_ref, q_ref, k_ref, v_ref, m2_ref, o_ref, *, tq):
    hp = pl.program_id(1)
    qi = pl.program_id(2)
    q = q_ref[...].astype(F32)
    m2 = m2_ref[...]
    lane = lax.broadcasted_iota(jnp.int32, (tq, LANES), 1)
    q2 = jnp.concatenate([jnp.where(lane < SB_DH, q, 0.0), jnp.where(lane >= SB_DH, q, 0.0)],
                         axis=0).astype(BF16)
    row1 = lax.broadcasted_iota(jnp.int32, (2 * tq, 1), 0)
    bias = jnp.where(row1 < tq, bias_ref[2 * hp], bias_ref[2 * hp + 1])
    row = lax.broadcasted_iota(jnp.int32, (2 * tq, tq), 0)
    col = lax.broadcasted_iota(jnp.int32, (2 * tq, tq), 1)
    diag_valid = col < (row & (tq - 1))

    def blk(j):
        start = pl.multiple_of(j * tq, tq)
        return k_ref[pl.ds(start, tq), :], v_ref[pl.ds(start, tq), :]

    kd, vd = blk(qi)
    carry, acc = _sb_tile(q2, kd, vd, bias, m2, jnp.zeros((2 * tq, 1), F32),
                          jnp.zeros((2 * tq, LANES), F32), diag_valid)

    def body(i, ca):
        start = pl.multiple_of((qi - 2 - 2 * i) * tq, tq)
        kb, vb = k_ref[pl.ds(start, 2 * tq), :], v_ref[pl.ds(start, 2 * tq), :]
        return _sb_tile_pair(q2, kb, vb, bias, m2, ca[0], ca[1])

    carry, acc = lax.fori_loop(0, qi // 2, body, (carry, acc))

    def last(ca):
        kb, vb = blk(0)
        return _sb_tile(q2, kb, vb, bias, m2, ca[0], ca[1], None)

    carry, acc = lax.cond(qi % 2 == 1, last, lambda ca: ca, (carry, acc))
    o_ref[...] = jnp.where(lane < SB_DH, acc[:tq], acc[tq:]).astype(o_ref.dtype)


def _suffix_matrix(t, parts):
    m = (np.arange(t)[:, None] > np.arange(t)[None, :]).astype(np.float32)
    return jnp.asarray(np.concatenate([m] * parts, axis=0), dtype=BF16)


def _sb_prompt(qb, kb, vb, bias, *, batch, seq, tq):
    n = qb.shape[0]
    nq = seq // tq
    return pl.pallas_call(
        functools.partial(_sb_prompt_kernel, tq=tq),
        grid=(batch, SB_W // LANES, nq),
        in_specs=[
            pl.BlockSpec(memory_space=pltpu.SMEM),
            pl.BlockSpec((tq, LANES), lambda b, h, i: (b * nq + i, h)),
            pl.BlockSpec((seq, LANES), lambda b, h, i: (b, h)),
            pl.BlockSpec((seq, LANES), lambda b, h, i: (b, h)),
            pl.BlockSpec((tq, tq), lambda b, h, i: (0, 0)),
        ],
        out_specs=pl.BlockSpec((tq, LANES), lambda b, h, i: (b * nq + i, h)),
        out_shape=jax.ShapeDtypeStruct((n, SB_W), BF16),
        compiler_params=_params(("parallel", "parallel", "arbitrary")),
    )(bias, qb, kb, vb, _suffix_matrix(tq, 1))


def _sb_sample_kernel(pt_ref, qbd_ref, kn_ref, vn_ref, bias_ref, m2_ref, *refs, group, n_new, n_heads):
    del pt_ref
    kp = refs[:group]
    vp = refs[group:2 * group]
    o_ref = refs[2 * group]
    carry_ref, acc_ref = refs[2 * group + 1:]
    p = pl.program_id(1)
    qbd = qbd_ref[...]
    bias = bias_ref[...]
    m2 = m2_ref[...]
    rows = n_new * n_heads
    width = 2 * PAGE
    n_pair = group // 2

    @pl.when(p == 0)
    def _():
        pad = jnp.zeros((width - SAMPLE_PAD_T, SB_W), F32)
        kn = jnp.concatenate([kn_ref[...], pad], axis=0).astype(BF16)
        vn = jnp.concatenate([vn_ref[...], pad], axis=0).astype(BF16)
        row = lax.broadcasted_iota(jnp.int32, (rows, width), 0)
        col = lax.broadcasted_iota(jnp.int32, (rows, width), 1)
        valid = col < row // n_heads
        carry, acc = _sb_tile(qbd, kn, vn, bias, m2, jnp.zeros((rows, 1), F32),
                              jnp.zeros((rows, SB_W), F32), valid)
        carry_ref[...] = carry
        acc_ref[...] = acc

    def pair(page_refs, j):
        low = page_refs[group - 1 - 2 * j][...]
        high = page_refs[group - 2 - 2 * j][...]
        return jnp.concatenate([low, high], axis=1).astype(BF16)

    logs = [_sb_logs(jnp.dot(qbd, pair(kp, j), preferred_element_type=F32) + bias) for j in range(n_pair)]
    split = [jnp.concatenate(_split_bf16(lk), axis=1) for _, lk in logs]
    sfx = jnp.dot(jnp.concatenate(split, axis=0), m2, preferred_element_type=F32)
    carry = carry_ref[...]
    carries = [None] * n_pair
    for j in reversed(range(n_pair)):
        carries[j] = carry
        carry = carry + jnp.sum(logs[j][1], axis=1, keepdims=True)
    acc = acc_ref[...]
    for j in range(n_pair):
        a = jnp.exp(logs[j][0] + sfx[j * rows:(j + 1) * rows] + carries[j]).astype(BF16)
        acc = acc + lax.dot_general(a, pair(vp, j), (((1,), (1,)), ((), ())), preferred_element_type=F32)
    carry_ref[...] = carry
    acc_ref[...] = acc

    @pl.when(p == pl.num_programs(1) - 1)
    def _():
        row = lax.broadcasted_iota(jnp.int32, (rows, SB_W), 0)
        col = lax.broadcasted_iota(jnp.int32, (rows, SB_W), 1)
        own = jnp.where(col // SB_DH == row % n_heads, acc, 0.0)
        o_ref[...] = jnp.zeros(o_ref.shape, o_ref.dtype)
        for qq in range(n_new):
            o_ref[qq:qq + 1, :] = jnp.sum(own[qq * n_heads:(qq + 1) * n_heads, :], axis=0, keepdims=True)


def _sb_sample(qbd, k_new, v_new, bias_rows, cache_k, cache_v, page_table, layer, *, n_new, group):
    bs, n_pages = page_table.shape
    n_heads = SB_W // SB_DH
    rows = n_new * n_heads
    steps = n_pages // group

    def page_spec(i):
        def index(b, p, pt):
            return (layer, pt[b, n_pages - 1 - (p * group + i)], 0, 0)
        return pl.BlockSpec((None, None, SB_W, PAGE), index)

    grid_spec = pltpu.PrefetchScalarGridSpec(
        num_scalar_prefetch=1,
        grid=(bs, steps),
        in_specs=[
            pl.BlockSpec((None, rows, SB_W), lambda b, p, pt: (b, 0, 0)),
            pl.BlockSpec((SAMPLE_PAD_T, SB_W), lambda b, p, pt: (b, 0)),
            pl.BlockSpec((SAMPLE_PAD_T, SB_W), lambda b, p, pt: (b, 0)),
            pl.BlockSpec((rows, 2 * PAGE), lambda b, p, pt: (0, 0)),
            pl.BlockSpec((4 * PAGE, 2 * PAGE), lambda b, p, pt: (0, 0)),
        ] + [page_spec(i) for i in range(group)] * 2,
        out_specs=pl.BlockSpec((SAMPLE_PAD_T, SB_W), lambda b, p, pt: (b, 0)),
        scratch_shapes=[pltpu.VMEM((rows, 1), F32), pltpu.VMEM((rows, SB_W), F32)],
    )
    return pl.pallas_call(
        functools.partial(_sb_sample_kernel, group=group, n_new=n_new, n_heads=n_heads),
        grid_spec=grid_spec,
        out_shape=jax.ShapeDtypeStruct((bs * SAMPLE_PAD_T, SB_W), F32),
        compiler_params=_params(("parallel", "arbitrary")),
    )(page_table, qbd, k_new, v_new, bias_rows, _suffix_matrix(2 * PAGE, 2),
      *([cache_k] * group), *([cache_v] * group))


def _lru_kernel(x_ref, conv0_ref, h0_ref, cw_ref, cb_ref, wg_ref, ba_ref, bx_ref, lam_ref,
                y_ref, conv_ref, hl_ref, xpad, a_s, b_s, hs_s, hcar, *, tc, n_valid):
    ci = pl.program_id(1)
    head = SUBLANES
    tail = CONV_TAPS - 1

    @pl.when(ci == 0)
    def _():
        xpad[head - tail:head, :] = conv0_ref[...]
        hcar[...] = h0_ref[...]

    xpad[head:head + tc, :] = x_ref[:, 0:LRU_W]
    xc = cb_ref[...] + xpad[head - tail:head - tail + tc, :] * cw_ref[0:1, :]
    for w in range(1, CONV_TAPS):
        xc = xc + xpad[head - tail + w:head - tail + w + tc, :] * cw_ref[w:w + 1, :]
    xcb = xc.astype(BF16)
    neg_sp = -LRU_SCALE * _softplus(-lam_ref[...])
    for j in range(LRU_W // LANES):
        sl = slice(j * LANES, (j + 1) * LANES)
        g = jnp.dot(xcb[:, sl], wg_ref[j], preferred_element_type=F32)
        r = jax.nn.sigmoid(g[:, :LANES] + ba_ref[:, sl])
        gate_in = jax.nn.sigmoid(g[:, LANES:] + bx_ref[:, sl])
        log_a = neg_sp[:, sl] * r
        a = jnp.exp(log_a)
        mult = jnp.sqrt(-jnp.tanh(log_a) * (a * a + 1.0))
        a_s[:, sl] = a
        b_s[:, sl] = mult * (gate_in * xc[:, sl])

    rowi = lax.broadcasted_iota(jnp.int32, (SUBLANES, LRU_W), 0)

    def tile_scan(g, h):
        r0 = pl.multiple_of(g * SUBLANES, SUBLANES)
        a = a_s[pl.ds(r0, SUBLANES), :]
        b = b_s[pl.ds(r0, SUBLANES), :]
        for d in (1, 2, 4):
            keep = rowi >= d
            b = jnp.where(keep, a * pltpu.roll(b, d, axis=0) + b, b)
            a = jnp.where(keep, a * pltpu.roll(a, d, axis=0), a)
        hs = a * h + b
        hs_s[pl.ds(r0, SUBLANES), :] = hs
        return hs[SUBLANES - 1:SUBLANES, :]

    hcar[...] = lax.fori_loop(0, tc // SUBLANES, tile_scan, hcar[...])
    y_ref[...] = (hs_s[...] * _gelu_tanh(x_ref[:, LRU_W:2 * LRU_W])).astype(y_ref.dtype)

    new_tail = xpad[head + n_valid - tail:head + n_valid, :]
    conv_ref[...] = new_tail
    hl_ref[...] = hs_s[n_valid - 1:n_valid, :]
    xpad[head - tail:head, :] = new_tail


def _lru(lru_in, conv0, h0, cw, cb, wg, ba, bx, lam, *, batch, tok_per_seq, tc, n_valid, out_dtype):
    n = lru_in.shape[0]
    nc = tok_per_seq // tc
    vec = pl.BlockSpec((1, LRU_W), lambda b, c: (0, 0))
    tail = CONV_TAPS - 1
    return pl.pallas_call(
        functools.partial(_lru_kernel, tc=tc, n_valid=n_valid),
        grid=(batch, nc),
        in_specs=[
            pl.BlockSpec((tc, 2 * LRU_W), lambda b, c: (b * nc + c, 0)),
            pl.BlockSpec((None, tail, LRU_W), lambda b, c: (b, 0, 0)),
            pl.BlockSpec((None, 1, LRU_W), lambda b, c: (b, 0, 0)),
            pl.BlockSpec((CONV_TAPS, LRU_W), lambda b, c: (0, 0)),
            vec,
            pl.BlockSpec((LRU_W // LANES, LANES, 2 * LANES), lambda b, c: (0, 0, 0)),
            vec, vec, vec,
        ],
        out_specs=[
            pl.BlockSpec((tc, LRU_W), lambda b, c: (b * nc + c, 0)),
            pl.BlockSpec((None, tail, LRU_W), lambda b, c: (b, 0, 0)),
            pl.BlockSpec((None, 1, LRU_W), lambda b, c: (b, 0, 0)),
        ],
        out_shape=[
            jax.ShapeDtypeStruct((n, LRU_W), out_dtype),
            jax.ShapeDtypeStruct((batch, tail, LRU_W), F32),
            jax.ShapeDtypeStruct((batch, 1, LRU_W), F32),
        ],
        scratch_shapes=[
            pltpu.VMEM((SUBLANES + tc, LRU_W), F32),
            pltpu.VMEM((tc, LRU_W), F32),
            pltpu.VMEM((tc, LRU_W), F32),
            pltpu.VMEM((tc, LRU_W), F32),
            pltpu.VMEM((1, LRU_W), F32),
        ],
        compiler_params=_params(("parallel", "arbitrary")),
    )(lru_in, conv0, h0, cw, cb, wg, ba, bx, lam)


def _gla_kernel(x_ref, s0_ref, wal_ref, bal_ref, gn_ref, tri_ref, y_ref, sout_ref, st_ref, *, c, n_valid, nseq):
    ci = pl.program_id(1)
    seqs = range(nseq)
    nt = (((1,), (1,)), ((), ()))

    @pl.when(ci == 0)
    def _():
        for s in seqs:
            st_ref[s] = s0_ref[s].T

    xs = [x_ref[s] for s in seqs]
    o_v = 2 * GLA_KW
    o_go = o_v + GLA_VW
    o_lr = o_go + GLA_VW
    glr = jnp.concatenate([x[:, o_lr:] for x in xs], axis=0).astype(BF16)
    xl = jnp.dot(glr, wal_ref[...], preferred_element_type=F32) + bal_ref[...]
    la = _log_sigmoid(xl) * (1.0 / GLA_TAU)
    ks = [x[:, GLA_KW:o_v] for x in xs]
    las = [la[s * c:(s + 1) * c] for s in seqs]
    if n_valid < c:
        rowi = lax.broadcasted_iota(jnp.int32, (c, GLA_KW), 0)
        las = [jnp.where(rowi < n_valid, l, 0.0) for l in las]
        ks = [jnp.where(rowi < n_valid, k, 0.0) for k in ks]
    parts = [_split_bf16(l) for l in las]
    tri = tri_ref[...]
    g_all = (jnp.dot(tri, jnp.concatenate([p[0] for p in parts], axis=1), preferred_element_type=F32)
             + jnp.dot(tri, jnp.concatenate([p[1] for p in parts], axis=1), preferred_element_type=F32))

    lane_k = lax.broadcasted_iota(jnp.int32, (1, GLA_KW), 1) // GLA_DK
    row = lax.broadcasted_iota(jnp.int32, (c, GLA_H * c), 0)
    col = lax.broadcasted_iota(jnp.int32, (c, GLA_H * c), 1)
    causal = col % c <= row
    col_h = col // c

    def by_head(a):
        return jnp.concatenate([jnp.where(lane_k == h, a, 0.0) for h in range(GLA_H)], axis=0)

    q_ins, q_rels, kbd_rels, kbd_decs, decays = [], [], [], [], []
    for s in seqs:
        g = g_all[:, s * GLA_KW:(s + 1) * GLA_KW]
        g_last = g[c - 1:c, :]
        g_mid = g[c // 2:c // 2 + 1, :]
        q = xs[s][:, 0:GLA_KW] * (GLA_DK ** -0.5)
        q_ins.append(q * jnp.exp(g))
        q_rels.append((q * jnp.exp(g - g_mid)).astype(BF16))
        kbd_rels.append(by_head(ks[s] * jnp.exp(g_mid - g)).astype(BF16))
        kbd_decs.append(by_head(ks[s] * jnp.exp(g_last - g)).astype(BF16))
        decays.append(jnp.exp(g_last))

    atts = [lax.dot_general(q_rels[s], kbd_rels[s], nt, preferred_element_type=F32) for s in seqs]
    v_ts, w_ts, lhss = [], [], []
    for s in seqs:
        v = xs[s][:, o_v:o_go]
        v_t = jnp.concatenate([v[:, h * GLA_DV:(h + 1) * GLA_DV] for h in range(GLA_H)], axis=0).T
        v_ts.append(v_t)
        w_ts.append(jnp.concatenate([st_ref[s], v_t], axis=1).astype(BF16))
        att = jnp.where(causal, atts[s], 0.0)
        lhss.append(jnp.concatenate(
            [jnp.concatenate([jnp.where(lane_k == h, q_ins[s], 0.0), jnp.where(col_h == h, att, 0.0)], axis=1)
             for h in range(GLA_H)], axis=0).astype(BF16))
    ress = [lax.dot_general(lhss[s], w_ts[s], nt, preferred_element_type=F32) for s in seqs]
    upds = [jnp.dot(v_ts[s].astype(BF16), kbd_decs[s], preferred_element_type=F32) for s in seqs]

    for s in seqs:
        st_new = st_ref[s] * decays[s] + upds[s]
        st_ref[s] = st_new
        outs = []
        for h in range(GLA_H):
            oh = ress[s][h * c:(h + 1) * c, :]
            oh = oh * lax.rsqrt(jnp.mean(oh * oh, axis=-1, keepdims=True) + EPS)
            outs.append(oh * gn_ref[:, h * GLA_DV:(h + 1) * GLA_DV])
        go = xs[s][:, o_go:o_lr]
        y_ref[s] = (jnp.concatenate(outs, axis=1) * _silu(go)).astype(y_ref.dtype)

    @pl.when(ci == pl.num_programs(1) - 1)
    def _():
        for s in seqs:
            sout_ref[s] = st_ref[s].T


def _gla(gla_in, s0, wal, bal, gn, *, batch, tok_per_seq, c, n_valid, out_dtype):
    nc = tok_per_seq // c
    nseq = _pick_tile(batch, 4)
    tri = jnp.asarray((np.arange(c)[:, None] >= np.arange(c)[None, :]).astype(np.float32), dtype=BF16)
    state = pl.BlockSpec((nseq, GLA_KW, GLA_DV), lambda b, i: (b, 0, 0))
    y, s_out = pl.pallas_call(
        functools.partial(_gla_kernel, c=c, n_valid=n_valid, nseq=nseq),
        grid=(batch // nseq, nc),
        in_specs=[
            pl.BlockSpec((nseq, c, GLA_IN_W), lambda b, i: (b, i, 0)),
            state,
            pl.BlockSpec((LANES, GLA_KW), lambda b, i: (0, 0)),
            pl.BlockSpec((1, GLA_KW), lambda b, i: (0, 0)),
            pl.BlockSpec((1, GLA_VW), lambda b, i: (0, 0)),
            pl.BlockSpec((c, c), lambda b, i: (0, 0)),
        ],
        out_specs=[pl.BlockSpec((nseq, c, GLA_VW), lambda b, i: (b, i, 0)), state],
        out_shape=[
            jax.ShapeDtypeStruct((batch, tok_per_seq, GLA_VW), out_dtype),
            jax.ShapeDtypeStruct((batch, GLA_KW, GLA_DV), F32),
        ],
        scratch_shapes=[pltpu.VMEM((nseq, GLA_DV, GLA_KW), F32)],
        compiler_params=_params(("parallel", "arbitrary")),
    )(gla_in.reshape(batch, tok_per_seq, GLA_IN_W), s0, wal, bal, gn, tri)
    return y.reshape(batch * tok_per_seq, GLA_VW), s_out


def _merge_kernel(x_ref, mod_ref, g_ref, ysb_ref, ylru_ref, ygla_ref, wg_ref, wbr_ref, wout_ref,
                  o_ref, *, per_token):
    x = x_ref[...]
    d = x.shape[1]
    h = _norm_mod(x, g_ref[...], _mod_row(mod_ref, 0, per_token),
                  _mod_row(mod_ref, 1, per_token)).astype(BF16)
    merged = None
    for b, y_ref in enumerate((ysb_ref, ylru_ref, ygla_ref)):
        gate = jax.nn.sigmoid(jnp.dot(h, wg_ref[:, b * d:(b + 1) * d], preferred_element_type=F32))
        br = jnp.dot(y_ref[...].astype(BF16), wbr_ref[b], preferred_element_type=F32)
        merged = gate * br if merged is None else merged + gate * br
    out = jnp.dot(merged.astype(BF16), wout_ref[...], preferred_element_type=F32)
    o_ref[...] = x + _mod_row(mod_ref, 2, per_token) * out


def _merge(x, mod, g, ysb, ylru, ygla, wg, wbr, wout, *, per_token, tok_per_seq, tm):
    n, d = x.shape
    row = lambda w: pl.BlockSpec((tm, w), lambda i: (i, 0))
    return pl.pallas_call(
        functools.partial(_merge_kernel, per_token=per_token),
        grid=(n // tm,),
        in_specs=[
            row(d),
            _mod_spec(per_token, tm, d, tok_per_seq),
            pl.BlockSpec((1, d), lambda i: (0, 0)),
            row(SB_W), row(LRU_W), row(GLA_VW),
            pl.BlockSpec((d, 3 * d), lambda i: (0, 0)),
            pl.BlockSpec((3, SB_W, d), lambda i: (0, 0, 0)),
            pl.BlockSpec((d, d), lambda i: (0, 0)),
        ],
        out_specs=row(d),
        out_shape=jax.ShapeDtypeStruct((n, d), F32),
        compiler_params=_params(("parallel",)),
    )(x, mod, g, ysb, ylru, ygla, wg, wbr, wout)


def _ffn_kernel(x_ref, mod_ref, g_ref, wgate_ref, wup_ref, wdown_ref, gfin_ref, o_ref, h_s, acc_s,
                *, per_token, final):
    f = pl.program_id(1)

    @pl.when(f == 0)
    def _():
        h_s[...] = _norm_mod(x_ref[...], g_ref[...], _mod_row(mod_ref, 3, per_token),
                             _mod_row(mod_ref, 4, per_token)).astype(BF16)
        acc_s[...] = jnp.zeros(acc_s.shape, F32)

    h = h_s[...]
    ff = _silu(jnp.dot(h, wgate_ref[...], preferred_element_type=F32)) * jnp.dot(
        h, wup_ref[...], preferred_element_type=F32)
    acc_s[...] += jnp.dot(ff.astype(BF16), wdown_ref[...], preferred_element_type=F32)

    @pl.when(f == pl.num_programs(1) - 1)
    def _():
        x2 = x_ref[...] + _mod_row(mod_ref, 5, per_token) * acc_s[...]
        if final:
            x2 = x2 * lax.rsqrt(jnp.mean(x2 * x2, axis=-1, keepdims=True) + EPS) * gfin_ref[...]
        o_ref[...] = x2


def _ffn(x, mod, g, wgate, wup, wdown, gfin, *, per_token, tok_per_seq, tm, final):
    n, d = x.shape
    dff = wgate.shape[1]
    tf = dff // 2 if (dff // 2) % LANES == 0 else dff
    mod_spec = _mod_spec(per_token, tm, d, tok_per_seq)
    return pl.pallas_call(
        functools.partial(_ffn_kernel, per_token=per_token, final=final),
        grid=(n // tm, dff // tf),
        in_specs=[
            pl.BlockSpec((tm, d), lambda i, f: (i, 0)),
            mod_spec,
            pl.BlockSpec((1, d), lambda i, f: (0, 0)),
            pl.BlockSpec((d, tf), lambda i, f: (0, f)),
            pl.BlockSpec((d, tf), lambda i, f: (0, f)),
            pl.BlockSpec((tf, d), lambda i, f: (f, 0)),
            pl.BlockSpec((1, d), lambda i, f: (0, 0)),
        ],
        out_specs=pl.BlockSpec((tm, d), lambda i, f: (i, 0)),
        out_shape=jax.ShapeDtypeStruct((n, d), F32),
        scratch_shapes=[pltpu.VMEM((tm, d), BF16), pltpu.VMEM((tm, d), F32)],
        compiler_params=_params(("parallel", "arbitrary")),
    )(x, mod, g, wgate, wup, wdown, gfin)


def _split_w_in(w_in_l):
    d = w_in_l.shape[0]
    sizes = (SB_W, SB_W, SB_W, LRU_W, LRU_W, GLA_KW, GLA_KW, GLA_VW, GLA_VW, GLA_RANK, d, d, d)
    offs = np.concatenate([[0], np.cumsum(sizes)])
    seg = lambda i: w_in_l[:, offs[i]:offs[i + 1]]
    glr = jnp.pad(seg(9), ((0, 0), (0, LANES - GLA_RANK)))
    w_mix = jnp.concatenate([seg(i) for i in range(9)] + [glr], axis=1).astype(BF16)
    w_gates = jnp.concatenate([seg(10), seg(11), seg(12)], axis=1).astype(BF16)
    return w_mix, w_gates


def _lru_gate_tiles(wa, wx):
    nb = wa.shape[0]
    z = jnp.zeros((LRU_BLK, LRU_BLK), F32)

    def pair(w, j):
        return jnp.concatenate([jnp.concatenate([w[2 * j], z], axis=1),
                                jnp.concatenate([z, w[2 * j + 1]], axis=1)], axis=0)

    tiles = [jnp.concatenate([pair(wa, j), pair(wx, j)], axis=1) for j in range(nb // 2)]
    return jnp.stack(tiles).astype(BF16)


def _pick_tile(n, pref):
    t = pref
    while n % t:
        t //= 2
    return t


def kernel(x_prompt, x_sample, c_prompt, c_sample, cache_k, cache_v, page_table, state_conv, state_lru, state_gla, w_ada, b_ada, norm_mix, w_in, sb_bias, conv_w, conv_b, lru_wa, lru_ba, lru_wx, lru_bx, lru_lambda, gla_w_alpha, gla_b_alpha, gla_norm, w_br_sb, w_br_lru, w_br_gla, w_out, norm_ffn, w_gate, w_up, w_down, norm_final):
    bp, seq, d = x_prompt.shape
    bs, n_new, _ = x_sample.shape
    depth = w_ada.shape[0]
    n_heads = SB_W // SB_DH
    n_pool = cache_k.shape[1]
    tsp = SAMPLE_PAD_T

    xp = x_prompt.reshape(bp * seq, d)
    xs = jnp.pad(x_sample, ((0, 0), (0, tsp - n_new), (0, 0))).reshape(bs * tsp, d)
    ck = jnp.transpose(cache_k, (0, 1, 3, 4, 2)).reshape(depth, n_pool, SB_W, PAGE)
    cv = jnp.transpose(cache_v, (0, 1, 3, 4, 2)).reshape(depth, n_pool, SB_W, PAGE)

    mod = _ada(jnp.concatenate([c_prompt, c_sample], axis=0), w_ada, b_ada)
    mod = mod.reshape(depth, bp + bs, 6, d)
    mod_p = mod[:, :bp]
    mod_s = jnp.transpose(jnp.repeat(mod[:, bp:], tsp, axis=1), (0, 2, 1, 3))

    tm_p = _pick_tile(seq, 512)
    tm_s = bs * tsp
    tq = _pick_tile(seq, 256)
    tc_p = _pick_tile(seq, 256)
    c_gla = _pick_tile(seq, 64)
    group = _pick_tile(page_table.shape[1], 16)

    zeros_conv = jnp.zeros((bp, CONV_TAPS - 1, LRU_W), F32)
    zeros_h = jnp.zeros((bp, 1, LRU_W), F32)
    zeros_s = jnp.zeros((bp, GLA_KW, GLA_DV), F32)
    row2 = lambda a: a.reshape(1, -1)

    outs = {k: [] for k in ("cvp", "hp", "sp", "ks", "vs", "cvs", "hs", "ss")}
    k_slab = v_slab = None
    for l in range(depth):
        w_mix, w_gates = _split_w_in(w_in[l])
        wg_lru = _lru_gate_tiles(lru_wa[l], lru_wx[l])
        wal = jnp.pad(gla_w_alpha[l], ((0, LANES - GLA_RANK), (0, 0))).astype(BF16)
        wbr = jnp.stack([w_br_sb[l], w_br_lru[l], w_br_gla[l]]).astype(BF16)
        wout = w_out[l].astype(BF16)
        wgate, wup, wdown = w_gate[l].astype(BF16), w_up[l].astype(BF16), w_down[l].astype(BF16)
        lru_args = (conv_w[l], row2(conv_b[l]), wg_lru, row2(lru_ba[l]), row2(lru_bx[l]), row2(lru_lambda[l]))
        gla_args = (wal, row2(gla_b_alpha[l]), row2(gla_norm[l]))
        final = l == depth - 1

        qb, k_slab, v_slab, kb, vb, lru_in, gla_in = _inproj(
            xp, mod_p[l], row2(norm_mix[l]), w_mix, per_token=False, tok_per_seq=seq, tm=tm_p,
            kv_slab=(l, depth, k_slab, v_slab))
        y_sb = _sb_prompt(qb, kb, vb, sb_bias[l], batch=bp, seq=seq, tq=tq)
        y_lru, cv1, h1 = _lru(lru_in, zeros_conv, zeros_h, *lru_args, batch=bp, tok_per_seq=seq,
                              tc=tc_p, n_valid=tc_p, out_dtype=BF16)
        y_gla, s1 = _gla(gla_in, zeros_s, *gla_args, batch=bp, tok_per_seq=seq, c=c_gla,
                         n_valid=c_gla, out_dtype=BF16)
        xp = _merge(xp, mod_p[l], row2(norm_mix[l]), y_sb, y_lru, y_gla, w_gates, wbr, wout,
                    per_token=False, tok_per_seq=seq, tm=tm_p)
        xp = _ffn(xp, mod_p[l], row2(norm_ffn[l]), wgate, wup, wdown, row2(norm_final),
                  per_token=False, tok_per_seq=seq, tm=tm_p, final=final)
        outs["cvp"].append(cv1)
        outs["hp"].append(h1.reshape(bp, LRU_W))
        outs["sp"].append(s1.reshape(bp, GLA_H, GLA_DK, GLA_DV))

        qb, k, v, _, _, lru_in, gla_in = _inproj(
            xs, mod_s[l], row2(norm_mix[l]), w_mix, per_token=True, tok_per_seq=tsp, tm=tm_s)
        q4 = qb.reshape(bs, tsp, n_heads, SB_DH)[:, :n_new]
        eye = jnp.eye(n_heads, dtype=BF16)
        qbd = (q4[:, :, None, :, :] * eye[None, None, :, :, None]).reshape(bs, n_new * n_heads, SB_W)
        bias_rows = jnp.broadcast_to(jnp.tile(sb_bias[l], n_new)[:, None], (n_new * n_heads, 2 * PAGE))
        y_sb = _sb_sample(qbd, k, v, bias_rows, ck, cv, page_table, l, n_new=n_new, group=group)
        y_lru, cv2, h2 = _lru(lru_in, state_conv[l], state_lru[l].reshape(bs, 1, LRU_W), *lru_args,
                              batch=bs, tok_per_seq=tsp, tc=tsp, n_valid=n_new, out_dtype=F32)
        gla_pad = jnp.pad(gla_in.reshape(bs, tsp, GLA_IN_W),
                          ((0, 0), (0, GLA_SAMPLE_C - tsp), (0, 0))).reshape(bs * GLA_SAMPLE_C, GLA_IN_W)
        y_gla, s2 = _gla(gla_pad, state_gla[l].reshape(bs, GLA_KW, GLA_DV), *gla_args, batch=bs,
                         tok_per_seq=GLA_SAMPLE_C, c=GLA_SAMPLE_C, n_valid=n_new, out_dtype=F32)
        y_gla = y_gla.reshape(bs, GLA_SAMPLE_C, GLA_VW)[:, :tsp].reshape(bs * tsp, GLA_VW)
        xs = _merge(xs, mod_s[l], row2(norm_mix[l]), y_sb, y_lru, y_gla, w_gates, wbr, wout,
                    per_token=True, tok_per_seq=tsp, tm=tm_s)
        xs = _ffn(xs, mod_s[l], row2(norm_ffn[l]), wgate, wup, wdown, row2(norm_final),
                  per_token=True, tok_per_seq=tsp, tm=tm_s, final=final)
        outs["ks"].append(k.reshape(bs, tsp, n_heads, SB_DH)[:, :n_new])
        outs["vs"].append(v.reshape(bs, tsp, n_heads, SB_DH)[:, :n_new])
        outs["cvs"].append(cv2)
        outs["hs"].append(h2.reshape(bs, LRU_W))
        outs["ss"].append(s2.reshape(bs, GLA_H, GLA_DK, GLA_DV))

    y_prompt = xp.reshape(bp, seq, d)
    y_sample = xs.reshape(bs, tsp, d)[:, :n_new]
    st = lambda name: jnp.stack(outs[name])
    rows = lambda slab: jnp.transpose(slab.reshape(depth, bp, n_heads, SB_DH, seq), (0, 1, 4, 2, 3))
    return (y_prompt, y_sample, rows(k_slab), rows(v_slab), st("cvp"), st("hp"), st("sp"),
            st("ks"), st("vs"), st("cvs"), st("hs"), st("ss"))
```

```python
import functools

import numpy as np
import jax
import jax.numpy as jnp
from jax import lax
from jax.experimental import pallas as pl
from jax.experimental.pallas import tpu as pltpu

F32 = jnp.float32
BF16 = jnp.bfloat16

SB_DH = 64
SB_W = 512
LRU_W = 512
LRU_BLK = 64
CONV_TAPS = 4
LRU_SCALE = 8.0
GLA_H = 4
GLA_DK = 64
GLA_DV = 128
GLA_KW = GLA_H * GLA_DK
GLA_VW = GLA_H * GLA_DV
GLA_RANK = 16
GLA_TAU = 16.0
EPS = 1e-6
PAGE = 128
SB_TILE = 256

LANES = 128
SUBLANES = 8
VMEM_LIMIT = 56 * 1024 * 1024

SAMPLE_PAD_T = 8
GLA_SAMPLE_C = 32
GLA_IN_W = GLA_KW * 2 + GLA_VW * 2 + LANES
MIX_W = 3 * SB_W + 2 * LRU_W + GLA_IN_W


def _params(sem):
    return pltpu.CompilerParams(dimension_semantics=sem, vmem_limit_bytes=VMEM_LIMIT)


def _softplus(x):
    return jnp.maximum(x, 0.0) + jnp.log1p(jnp.exp(-jnp.abs(x)))


def _log_sigmoid(x):
    return jnp.minimum(x, 0.0) - jnp.log1p(jnp.exp(-jnp.abs(x)))


def _silu(x):
    return x * jax.nn.sigmoid(x)


def _gelu_tanh(x):
    return 0.5 * x * (1.0 + jnp.tanh(0.7978845608028654 * (x + 0.044715 * (x * x * x))))


def _split_bf16(x):
    hi = x.astype(BF16)
    lo = (x - hi.astype(F32)).astype(BF16)
    return hi, lo


def _mod_row(mod_ref, idx, per_token):
    if per_token:
        return mod_ref[idx]
    return mod_ref[idx:idx + 1, :]


def _norm_mod(x, g, shift, scale):
    xn = x * lax.rsqrt(jnp.mean(x * x, axis=-1, keepdims=True) + EPS) * g
    return xn * (1.0 + scale) + shift


def _mod_spec(per_token, tm, d, tok_per_seq):
    if per_token:
        return pl.BlockSpec((6, tm, d), lambda i, *_: (0, i, 0))
    blocks_per_seq = tok_per_seq // tm
    return pl.BlockSpec((None, 6, d), lambda i, *_: (i // blocks_per_seq, 0, 0))


def _ada_kernel(c_ref, w_ref, b_ref, o_ref):
    c = c_ref[...]
    s = _silu(c).astype(BF16)
    o_ref[...] = jnp.dot(s, w_ref[...].astype(BF16), preferred_element_type=F32) + b_ref[...]


def _ada(c_all, w_ada, b_ada):
    depth, d, e = w_ada.shape
    n = c_all.shape[0]
    tn = 1536 if e % 1536 == 0 else e
    return pl.pallas_call(
        _ada_kernel,
        grid=(depth, e // tn),
        in_specs=[
            pl.BlockSpec((n, d), lambda l, j: (0, 0)),
            pl.BlockSpec((None, d, tn), lambda l, j: (l, 0, j)),
            pl.BlockSpec((None, 1, tn), lambda l, j: (l, 0, j)),
        ],
        out_specs=pl.BlockSpec((None, n, tn), lambda l, j: (l, 0, j)),
        out_shape=jax.ShapeDtypeStruct((depth, n, e), F32),
        compiler_params=_params(("parallel", "parallel")),
    )(c_all, w_ada, b_ada.reshape(depth, 1, e))


def _inproj_kernel(x_ref, mod_ref, g_ref, w_ref, *rest, per_token, kv_slab):
    q_ref, k_ref, v_ref, kb_ref, vb_ref, lru_ref, gla_ref = rest[-7:]
    h = _norm_mod(x_ref[...], g_ref[...], _mod_row(mod_ref, 0, per_token),
                  _mod_row(mod_ref, 1, per_token)).astype(BF16)

    def seg(a, b):
        return jnp.dot(h, w_ref[:, a:b], preferred_element_type=F32)

    q = seg(0, SB_W) * (SB_DH ** -0.5)
    k = seg(SB_W, 2 * SB_W)
    v = seg(2 * SB_W, 3 * SB_W)
    if kv_slab:
        perm, perm_t = rest[0][...], rest[1][...]
        q_ref[...] = q.T.astype(BF16)
        k_ref[...] = k.T
        v_t = v.T
        v_ref[...] = v_t
        kb = k.astype(BF16)
        vb = v.astype(BF16)
        for t in range(k.shape[0] // SB_TILE):
            sl = slice(t * SB_TILE, (t + 1) * SB_TILE)
            kb_ref[sl, :] = jnp.dot(perm, kb[sl, :], preferred_element_type=F32).astype(BF16)
            vb_ref[t] = lax.dot_general(vb[sl, :], perm_t, (((0,), (0,)), ((), ())),
                                        preferred_element_type=F32).astype(BF16)
    else:
        q_ref[...] = q.astype(BF16)
        k_ref[...] = k
        kb_ref[...] = k.astype(BF16)
        v_ref[...] = v
        vb_ref[...] = v.astype(BF16)
    o = 3 * SB_W
    lru_ref[...] = seg(o, o + 2 * LRU_W)
    o += 2 * LRU_W
    gla_ref[...] = seg(o, o + GLA_IN_W)


def _key_permutation():
    groups = SB_TILE // SUBLANES
    p = np.zeros((SB_TILE, SB_TILE), np.float32)
    for r in range(groups):
        for sub in range(SUBLANES):
            p[SUBLANES * r + sub, groups * sub + r] = 1.0
    return jnp.asarray(p, dtype=BF16)


def _inproj(x, mod, g, w_mix, *, per_token, tok_per_seq, tm, kv_slab=None):
    n, d = x.shape
    row = lambda w: pl.BlockSpec((tm, w), lambda i: (i, 0))
    outs = [(SB_W, BF16), (SB_W, F32), (SB_W, F32), (SB_W, BF16), (SB_W, BF16),
            (2 * LRU_W, F32), (GLA_IN_W, F32)]
    out_specs = [row(w) for w, _ in outs]
    out_shape = [jax.ShapeDtypeStruct((n, w), dt) for w, dt in outs]
    extra_in, extra_specs, aliases = [], [], {}
    if kv_slab is not None:
        layer, depth, k_slab, v_slab = kv_slab
        bps = tok_per_seq // tm
        batch = n // tok_per_seq
        slab_spec = pl.BlockSpec((None, None, SB_W, tm), lambda i: (layer, i // bps, 0, i % bps))
        slab_shape = jax.ShapeDtypeStruct((depth, batch, SB_W, tok_per_seq), F32)
        out_specs[1] = out_specs[2] = slab_spec
        out_shape[1] = out_shape[2] = slab_shape
        out_specs[0] = pl.BlockSpec((None, SB_W, tm), lambda i: (i // bps, 0, i % bps))
        out_shape[0] = jax.ShapeDtypeStruct((batch, SB_W, tok_per_seq), BF16)
        tpb = tm // SB_TILE
        out_specs[4] = pl.BlockSpec((None, tpb, SB_W, SB_TILE), lambda i: (i // bps, i % bps, 0, 0))
        out_shape[4] = jax.ShapeDtypeStruct((batch, tok_per_seq // SB_TILE, SB_W, SB_TILE), BF16)
        perm = _key_permutation()
        extra_in = [perm, perm.T]
        extra_specs = [pl.BlockSpec((SB_TILE, SB_TILE), lambda i: (0, 0))] * 2
        if k_slab is not None:
            extra_in += [k_slab, v_slab]
            extra_specs += [pl.BlockSpec(memory_space=pl.ANY)] * 2
            aliases = {6: 1, 7: 2}
    return pl.pallas_call(
        functools.partial(_inproj_kernel, per_token=per_token, kv_slab=kv_slab is not None),
        grid=(n // tm,),
        in_specs=[
            row(d),
            _mod_spec(per_token, tm, d, tok_per_seq),
            pl.BlockSpec((1, d), lambda i: (0, 0)),
            pl.BlockSpec((d, MIX_W), lambda i: (0, 0)),
        ] + extra_specs,
        out_specs=out_specs,
        out_shape=out_shape,
        input_output_aliases=aliases,
        compiler_params=_params(("parallel",)),
    )(x, mod, g, w_mix, *extra_in)


def _sb_logs(z):
    t = jnp.log(1.0 + jnp.exp(-jnp.abs(z)))
    m = jnp.minimum(z, 0.0)
    return m - t, (m - z) - t


def _sb_logs_neg(z):
    neg_log_keep = jnp.maximum(z, 0.0) + jnp.log(1.0 + jnp.exp(-jnp.abs(z)))
    return z - neg_log_keep, neg_log_keep


def _suffix_lhs(log_keep, m2):
    if m2.shape[0] == log_keep.shape[1]:
        return log_keep.astype(BF16)
    return jnp.concatenate(_split_bf16(log_keep), axis=1)


def _sb_tile(qh, kblk, vblk, bias, m2, carry, acc, valid):
    z = lax.dot_general(qh, kblk, (((1,), (1,)), ((), ())), preferred_element_type=F32) + bias
    log_beta, log_keep = _sb_logs(z)
    if valid is not None:
        log_keep = jnp.where(valid, log_keep, 0.0)
    sfx = jnp.dot(_suffix_lhs(log_keep, m2), m2, preferred_element_type=F32)
    a = jnp.exp(log_beta + sfx + carry)
    if valid is not None:
        a = jnp.where(valid, a, 0.0)
    carry = carry + jnp.sum(log_keep, axis=1, keepdims=True)
    acc = acc + jnp.dot(a.astype(BF16), vblk, preferred_element_type=F32)
    return carry, acc


def _sb_tiles_keys_major(q2t, k_rows, vt_cols, bias_row, carry, acc, valid):
    n_tiles = k_rows.shape[0] // SB_TILE
    groups = SB_TILE // SUBLANES
    width = q2t.shape[1]
    zt = jnp.dot(k_rows, q2t, preferred_element_type=F32) + bias_row
    log_beta, log_keep = _sb_logs_neg(zt)
    if valid is not None:
        log_keep = jnp.where(valid, log_keep, 0.0)
    sub = lax.broadcasted_iota(jnp.int32, (SUBLANES, width), 0)
    a_tiles = [None] * n_tiles
    for t in reversed(range(n_tiles)):
        rows = [slice(t * SB_TILE + SUBLANES * r, t * SB_TILE + SUBLANES * (r + 1)) for r in range(groups)]
        below = [None] * groups
        run = jnp.zeros((SUBLANES, width), F32)
        for r in reversed(range(groups)):
            below[r] = run
            run = run + log_keep[rows[r]]
        inc = run
        for d in (1, 2, 4):
            inc = inc + jnp.where(sub < SUBLANES - d, pltpu.roll(inc, SUBLANES - d, axis=0), 0.0)
        later = (inc - run) + carry
        carry = carry + jnp.broadcast_to(inc[0:1], (SUBLANES, width))
        a_tiles[t] = jnp.concatenate(
            [jnp.exp(log_beta[rows[r]] - below[r] - later) for r in range(groups)], axis=0)
    a = a_tiles[0] if n_tiles == 1 else jnp.concatenate(a_tiles, axis=0)
    if valid is not None:
        a = jnp.where(valid, a, 0.0)
    acc = acc + jnp.dot(vt_cols, a.astype(BF16), preferred_element_type=F32)
    return carry, acc


def _sb_prompt_kernel(bias_ref, qt_ref, k_ref, vt_ref, o_ref, *, tq):
    hp = pl.program_id(1)
    qi = pl.program_id(2)
    qt = qt_ref[...].astype(F32)
    feat = lax.broadcasted_iota(jnp.int32, (LANES, tq), 0)
    q2t = jnp.concatenate([jnp.where(feat < SB_DH, qt, 0.0), jnp.where(feat >= SB_DH, qt, 0.0)],
                          axis=1).astype(BF16)
    lane1 = lax.broadcasted_iota(jnp.int32, (1, 2 * tq), 1)
    bias_row = jnp.where(lane1 < tq, bias_ref[2 * hp], bias_ref[2 * hp + 1])
    row = lax.broadcasted_iota(jnp.int32, (tq, 2 * tq), 0)
    lane = lax.broadcasted_iota(jnp.int32, (tq, 2 * tq), 1)
    key = (row & (SUBLANES - 1)) * (tq // SUBLANES) + (row >> 3)
    diag_valid = key < (lane & (tq - 1))

    def k_rows(j, n):
        return k_ref[pl.ds(pl.multiple_of(j * tq, tq), n * tq), :]

    carry, acc = _sb_tiles_keys_major(q2t, k_rows(qi, 1), vt_ref[qi], bias_row,
                                      jnp.zeros((SUBLANES, 2 * tq), F32),
                                      jnp.zeros((LANES, 2 * tq), F32), diag_valid)

    def body(i, ca):
        j = qi - 2 - 2 * i
        vt = jnp.concatenate([vt_ref[j], vt_ref[j + 1]], axis=1)
        return _sb_tiles_keys_major(q2t, k_rows(j, 2), vt, bias_row, ca[0], ca[1], None)

    carry, acc = lax.fori_loop(0, qi // 2, body, (carry, acc))

    def last(ca):
        return _sb_tiles_keys_major(q2t, k_rows(0, 1), vt_ref[0], bias_row, ca[0], ca[1], None)

    carry, acc = lax.cond(qi % 2 == 1, last, lambda ca: ca, (carry, acc))
    out_t = jnp.concatenate([acc[:SB_DH, :tq], acc[SB_DH:, tq:]], axis=0)
    o_ref[...] = out_t.T.astype(o_ref.dtype)


def _suffix_matrix(t, parts):
    m = (np.arange(t)[:, None] > np.arange(t)[None, :]).astype(np.float32)
    return jnp.asarray(np.concatenate([m] * parts, axis=0), dtype=BF16)


def _sb_prompt(qt, kperm, vt_tiles, bias, *, batch, seq):
    tq = SB_TILE
    nq = seq // tq
    return pl.pallas_call(
        functools.partial(_sb_prompt_kernel, tq=tq),
        grid=(batch, SB_W // LANES, nq),
        in_specs=[
            pl.BlockSpec(memory_space=pltpu.SMEM),
            pl.BlockSpec((None, LANES, tq), lambda b, h, i: (b, h, i)),
            pl.BlockSpec((seq, LANES), lambda b, h, i: (b, h)),
            pl.BlockSpec((None, nq, LANES, tq), lambda b, h, i: (b, 0, h, 0)),
        ],
        out_specs=pl.BlockSpec((tq, LANES), lambda b, h, i: (b * nq + i, h)),
        out_shape=jax.ShapeDtypeStruct((batch * seq, SB_W), BF16),
        compiler_params=_params(("parallel", "parallel", "arbitrary")),
    )(bias, qt, kperm, vt_tiles)


def _sb_sample_kernel(pt_ref, qbd_ref, kn_ref, vn_ref, bias_ref, m2_ref, *refs, group, n_new, n_heads):
    del pt_ref
    kp = refs[:group]
    vp = refs[group:2 * group]
    o_ref = refs[2 * group]
    carry_ref, acc_ref = refs[2 * group + 1:]
    p = pl.program_id(1)
    qbd = qbd_ref[...]
    bias = bias_ref[...]
    m2 = m2_ref[...]
    rows = n_new * n_heads
    width = 2 * PAGE
    n_pair = group // 2

    @pl.when(p == 0)
    def _():
        pad = jnp.zeros((width - SAMPLE_PAD_T, SB_W), F32)
        kn = jnp.concatenate([kn_ref[...], pad], axis=0).astype(BF16)
        vn = jnp.concatenate([vn_ref[...], pad], axis=0).astype(BF16)
        row = lax.broadcasted_iota(jnp.int32, (rows, width), 0)
        col = lax.broadcasted_iota(jnp.int32, (rows, width), 1)
        valid = col < row // n_heads
        carry, acc = _sb_tile(qbd, kn, vn, bias, m2, jnp.zeros((rows, 1), F32),
                              jnp.zeros((rows, SB_W), F32), valid)
        carry_ref[...] = carry
        acc_ref[...] = acc

    def pair(page_refs, j):
        low = page_refs[group - 1 - 2 * j][...]
        high = page_refs[group - 2 - 2 * j][...]
        return jnp.concatenate([low, high], axis=1).astype(BF16)

    logs = [_sb_logs(jnp.dot(qbd, pair(kp, j), preferred_element_type=F32) + bias) for j in range(n_pair)]
    split = [jnp.concatenate(_split_bf16(lk), axis=1) for _, lk in logs]
    sfx = jnp.dot(jnp.concatenate(split, axis=0), m2, preferred_element_type=F32)
    carry = carry_ref[...]
    carries = [None] * n_pair
    for j in reversed(range(n_pair)):
        carries[j] = carry
        carry = carry + jnp.sum(logs[j][1], axis=1, keepdims=True)
    acc = acc_ref[...]
    for j in range(n_pair):
        a = jnp.exp(logs[j][0] + sfx[j * rows:(j + 1) * rows] + carries[j]).astype(BF16)
        acc = acc + lax.dot_general(a, pair(vp, j), (((1,), (1,)), ((), ())), preferred_element_type=F32)
    carry_ref[...] = carry
    acc_ref[...] = acc

    @pl.when(p == pl.num_programs(1) - 1)
    def _():
        row = lax.broadcasted_iota(jnp.int32, (rows, SB_W), 0)
        col = lax.broadcasted_iota(jnp.int32, (rows, SB_W), 1)
        own = jnp.where(col // SB_DH == row % n_heads, acc, 0.0)
        o_ref[...] = jnp.zeros(o_ref.shape, o_ref.dtype)
        for qq in range(n_new):
            o_ref[qq:qq + 1, :] = jnp.sum(own[qq * n_heads:(qq + 1) * n_heads, :], axis=0, keepdims=True)


def _sb_sample(qbd, k_new, v_new, bias_rows, cache_k, cache_v, page_table, layer, *, n_new, group):
    bs, n_pages = page_table.shape
    n_heads = SB_W // SB_DH
    rows = n_new * n_heads
    steps = n_pages // group

    def page_spec(i):
        def index(b, p, pt):
            return (layer, pt[b, n_pages - 1 - (p * group + i)], 0, 0)
        return pl.BlockSpec((None, None, SB_W, PAGE), index)

    grid_spec = pltpu.PrefetchScalarGridSpec(
        num_scalar_prefetch=1,
        grid=(bs, steps),
        in_specs=[
            pl.BlockSpec((None, rows, SB_W), lambda b, p, pt: (b, 0, 0)),
            pl.BlockSpec((SAMPLE_PAD_T, SB_W), lambda b, p, pt: (b, 0)),
            pl.BlockSpec((SAMPLE_PAD_T, SB_W), lambda b, p, pt: (b, 0)),
            pl.BlockSpec((rows, 2 * PAGE), lambda b, p, pt: (0, 0)),
            pl.BlockSpec((4 * PAGE, 2 * PAGE), lambda b, p, pt: (0, 0)),
        ] + [page_spec(i) for i in range(group)] * 2,
        out_specs=pl.BlockSpec((SAMPLE_PAD_T, SB_W), lambda b, p, pt: (b, 0)),
        scratch_shapes=[pltpu.VMEM((rows, 1), F32), pltpu.VMEM((rows, SB_W), F32)],
    )
    return pl.pallas_call(
        functools.partial(_sb_sample_kernel, group=group, n_new=n_new, n_heads=n_heads),
        grid_spec=grid_spec,
        out_shape=jax.ShapeDtypeStruct((bs * SAMPLE_PAD_T, SB_W), F32),
        compiler_params=_params(("parallel", "arbitrary")),
    )(page_table, qbd, k_new, v_new, bias_rows, _suffix_matrix(2 * PAGE, 2),
      *([cache_k] * group), *([cache_v] * group))


def _lru_kernel(x_ref, conv0_ref, h0_ref, cw_ref, cb_ref, wg_ref, ba_ref, bx_ref, lam_ref,
                y_ref, conv_ref, hl_ref, xpad, a_s, b_s, hs_s, hcar, *, tc, n_valid):
    ci = pl.program_id(1)
    head = SUBLANES
    tail = CONV_TAPS - 1

    @pl.when(ci == 0)
    def _():
        xpad[head - tail:head, :] = conv0_ref[...]
        hcar[...] = h0_ref[...]

    xpad[head:head + tc, :] = x_ref[:, 0:LRU_W]
    xc = cb_ref[...] + xpad[head - tail:head - tail + tc, :] * cw_ref[0:1, :]
    for w in range(1, CONV_TAPS):
        xc = xc + xpad[head - tail + w:head - tail + w + tc, :] * cw_ref[w:w + 1, :]
    xcb = xc.astype(BF16)
    neg_sp = -LRU_SCALE * _softplus(-lam_ref[...])
    for j in range(LRU_W // LANES):
        sl = slice(j * LANES, (j + 1) * LANES)
        g = jnp.dot(xcb[:, sl], wg_ref[j], preferred_element_type=F32)
        r = jax.nn.sigmoid(g[:, :LANES] + ba_ref[:, sl])
        gate_in = jax.nn.sigmoid(g[:, LANES:] + bx_ref[:, sl])
        log_a = neg_sp[:, sl] * r
        a = jnp.exp(log_a)
        mult = jnp.sqrt(-jnp.tanh(log_a) * (a * a + 1.0))
        a_s[:, sl] = a
        b_s[:, sl] = mult * (gate_in * xc[:, sl])

    rowi = lax.broadcasted_iota(jnp.int32, (SUBLANES, LRU_W), 0)

    def tile_scan(g, h):
        r0 = pl.multiple_of(g * SUBLANES, SUBLANES)
        a = a_s[pl.ds(r0, SUBLANES), :]
        b = b_s[pl.ds(r0, SUBLANES), :]
        for d in (1, 2, 4):
            keep = rowi >= d
            b = jnp.where(keep, a * pltpu.roll(b, d, axis=0) + b, b)
            a = jnp.where(keep, a * pltpu.roll(a, d, axis=0), a)
        hs = a * h + b
        hs_s[pl.ds(r0, SUBLANES), :] = hs
        return hs[SUBLANES - 1:SUBLANES, :]

    hcar[...] = lax.fori_loop(0, tc // SUBLANES, tile_scan, hcar[...])
    y_ref[...] = (hs_s[...] * _gelu_tanh(x_ref[:, LRU_W:2 * LRU_W])).astype(y_ref.dtype)

    new_tail = xpad[head + n_valid - tail:head + n_valid, :]
    conv_ref[...] = new_tail
    hl_ref[...] = hs_s[n_valid - 1:n_valid, :]
    xpad[head - tail:head, :] = new_tail


def _lru(lru_in, conv0, h0, cw, cb, wg, ba, bx, lam, *, batch, tok_per_seq, tc, n_valid, out_dtype):
    n = lru_in.shape[0]
    nc = tok_per_seq // tc
    vec = pl.BlockSpec((1, LRU_W), lambda b, c: (0, 0))
    tail = CONV_TAPS - 1
    return pl.pallas_call(
        functools.partial(_lru_kernel, tc=tc, n_valid=n_valid),
        grid=(batch, nc),
        in_specs=[
            pl.BlockSpec((tc, 2 * LRU_W), lambda b, c: (b * nc + c, 0)),
            pl.BlockSpec((None, tail, LRU_W), lambda b, c: (b, 0, 0)),
            pl.BlockSpec((None, 1, LRU_W), lambda b, c: (b, 0, 0)),
            pl.BlockSpec((CONV_TAPS, LRU_W), lambda b, c: (0, 0)),
            vec,
            pl.BlockSpec((LRU_W // LANES, LANES, 2 * LANES), lambda b, c: (0, 0, 0)),
            vec, vec, vec,
        ],
        out_specs=[
            pl.BlockSpec((tc, LRU_W), lambda b, c: (b * nc + c, 0)),
            pl.BlockSpec((None, tail, LRU_W), lambda b, c: (b, 0, 0)),
            pl.BlockSpec((None, 1, LRU_W), lambda b, c: (b, 0, 0)),
        ],
        out_shape=[
            jax.ShapeDtypeStruct((n, LRU_W), out_dtype),
            jax.ShapeDtypeStruct((batch, tail, LRU_W), F32),
            jax.ShapeDtypeStruct((batch, 1, LRU_W), F32),
        ],
        scratch_shapes=[
            pltpu.VMEM((SUBLANES + tc, LRU_W), F32),
            pltpu.VMEM((tc, LRU_W), F32),
            pltpu.VMEM((tc, LRU_W), F32),
            pltpu.VMEM((tc, LRU_W), F32),
            pltpu.VMEM((1, LRU_W), F32),
        ],
        compiler_params=_params(("parallel", "arbitrary")),
    )(lru_in, conv0, h0, cw, cb, wg, ba, bx, lam)


def _gla_kernel(x_ref, s0_ref, wal_ref, bal_ref, gn_ref, tri_ref, y_ref, sout_ref, st_ref, *, c, n_valid, nseq):
    ci = pl.program_id(1)
    seqs = range(nseq)
    nt = (((1,), (1,)), ((), ()))

    @pl.when(ci == 0)
    def _():
        for s in seqs:
            st_ref[s] = s0_ref[s].T

    xs = [x_ref[s] for s in seqs]
    o_v = 2 * GLA_KW
    o_go = o_v + GLA_VW
    o_lr = o_go + GLA_VW
    glr = jnp.concatenate([x[:, o_lr:] for x in xs], axis=0).astype(BF16)
    xl = jnp.dot(glr, wal_ref[...], preferred_element_type=F32) + bal_ref[...]
    la = _log_sigmoid(xl) * (1.0 / GLA_TAU)
    ks = [x[:, GLA_KW:o_v] for x in xs]
    las = [la[s * c:(s + 1) * c] for s in seqs]
    if n_valid < c:
        rowi = lax.broadcasted_iota(jnp.int32, (c, GLA_KW), 0)
        las = [jnp.where(rowi < n_valid, l, 0.0) for l in las]
        ks = [jnp.where(rowi < n_valid, k, 0.0) for k in ks]
    parts = [_split_bf16(l) for l in las]
    tri = tri_ref[...]
    g_all = (jnp.dot(tri, jnp.concatenate([p[0] for p in parts], axis=1), preferred_element_type=F32)
             + jnp.dot(tri, jnp.concatenate([p[1] for p in parts], axis=1), preferred_element_type=F32))

    lane_k = lax.broadcasted_iota(jnp.int32, (1, GLA_KW), 1) // GLA_DK
    row = lax.broadcasted_iota(jnp.int32, (c, GLA_H * c), 0)
    col = lax.broadcasted_iota(jnp.int32, (c, GLA_H * c), 1)
    causal = col % c <= row
    col_h = col // c

    def by_head(a):
        return jnp.concatenate([jnp.where(lane_k == h, a, 0.0) for h in range(GLA_H)], axis=0)

    q_ins, q_rels, kbd_rels, kbd_decs, decays = [], [], [], [], []
    for s in seqs:
        g = g_all[:, s * GLA_KW:(s + 1) * GLA_KW]
        g_last = g[c - 1:c, :]
        g_mid = g[c // 2:c // 2 + 1, :]
        q = xs[s][:, 0:GLA_KW] * (GLA_DK ** -0.5)
        q_ins.append(q * jnp.exp(g))
        q_rels.append((q * jnp.exp(g - g_mid)).astype(BF16))
        kbd_rels.append(by_head(ks[s] * jnp.exp(g_mid - g)).astype(BF16))
        kbd_decs.append(by_head(ks[s] * jnp.exp(g_last - g)).astype(BF16))
        decays.append(jnp.exp(g_last))

    atts = [lax.dot_general(q_rels[s], kbd_rels[s], nt, preferred_element_type=F32) for s in seqs]
    v_ts, w_ts, lhss = [], [], []
    for s in seqs:
        v = xs[s][:, o_v:o_go]
        v_t = jnp.concatenate([v[:, h * GLA_DV:(h + 1) * GLA_DV] for h in range(GLA_H)], axis=0).T
        v_ts.append(v_t)
        w_ts.append(jnp.concatenate([st_ref[s], v_t], axis=1).astype(BF16))
        att = jnp.where(causal, atts[s], 0.0)
        lhss.append(jnp.concatenate(
            [jnp.concatenate([jnp.where(lane_k == h, q_ins[s], 0.0), jnp.where(col_h == h, att, 0.0)], axis=1)
             for h in range(GLA_H)], axis=0).astype(BF16))
    ress = [lax.dot_general(lhss[s], w_ts[s], nt, preferred_element_type=F32) for s in seqs]
    upds = [jnp.dot(v_ts[s].astype(BF16), kbd_decs[s], preferred_element_type=F32) for s in seqs]

    for s in seqs:
        st_new = st_ref[s] * decays[s] + upds[s]
        st_ref[s] = st_new
        outs = []
        for h in range(GLA_H):
            oh = ress[s][h * c:(h + 1) * c, :]
            oh = oh * lax.rsqrt(jnp.mean(oh * oh, axis=-1, keepdims=True) + EPS)
            outs.append(oh * gn_ref[:, h * GLA_DV:(h + 1) * GLA_DV])
        go = xs[s][:, o_go:o_lr]
        y_ref[s] = (jnp.concatenate(outs, axis=1) * _silu(go)).astype(y_ref.dtype)

    @pl.when(ci == pl.num_programs(1) - 1)
    def _():
        for s in seqs:
            sout_ref[s] = st_ref[s].T


def _gla(gla_in, s0, wal, bal, gn, *, batch, tok_per_seq, c, n_valid, out_dtype):
    nc = tok_per_seq // c
    nseq = _pick_tile(batch, 4)
    tri = jnp.asarray((np.arange(c)[:, None] >= np.arange(c)[None, :]).astype(np.float32), dtype=BF16)
    state = pl.BlockSpec((nseq, GLA_KW, GLA_DV), lambda b, i: (b, 0, 0))
    y, s_out = pl.pallas_call(
        functools.partial(_gla_kernel, c=c, n_valid=n_valid, nseq=nseq),
        grid=(batch // nseq, nc),
        in_specs=[
            pl.BlockSpec((nseq, c, GLA_IN_W), lambda b, i: (b, i, 0)),
            state,
            pl.BlockSpec((LANES, GLA_KW), lambda b, i: (0, 0)),
            pl.BlockSpec((1, GLA_KW), lambda b, i: (0, 0)),
            pl.BlockSpec((1, GLA_VW), lambda b, i: (0, 0)),
            pl.BlockSpec((c, c), lambda b, i: (0, 0)),
        ],
        out_specs=[pl.BlockSpec((nseq, c, GLA_VW), lambda b, i: (b, i, 0)), state],
        out_shape=[
            jax.ShapeDtypeStruct((batch, tok_per_seq, GLA_VW), out_dtype),
            jax.ShapeDtypeStruct((batch, GLA_KW, GLA_DV), F32),
        ],
        scratch_shapes=[pltpu.VMEM((nseq, GLA_DV, GLA_KW), F32)],
        compiler_params=_params(("parallel", "arbitrary")),
    )(gla_in.reshape(batch, tok_per_seq, GLA_IN_W), s0, wal, bal, gn, tri)
    return y.reshape(batch * tok_per_seq, GLA_VW), s_out


def _merge_kernel(x_ref, mod_ref, g_ref, ysb_ref, ylru_ref, ygla_ref, wg_ref, wbr_ref, wout_ref,
                  o_ref, *, per_token):
    x = x_ref[...]
    d = x.shape[1]
    h = _norm_mod(x, g_ref[...], _mod_row(mod_ref, 0, per_token),
                  _mod_row(mod_ref, 1, per_token)).astype(BF16)
    merged = None
    for b, y_ref in enumerate((ysb_ref, ylru_ref, ygla_ref)):
        gate = jax.nn.sigmoid(jnp.dot(h, wg_ref[:, b * d:(b + 1) * d], preferred_element_type=F32))
        br = jnp.dot(y_ref[...].astype(BF16), wbr_ref[b], preferred_element_type=F32)
        merged = gate * br if merged is None else merged + gate * br
    out = jnp.dot(merged.astype(BF16), wout_ref[...], preferred_element_type=F32)
    o_ref[...] = x + _mod_row(mod_ref, 2, per_token) * out


def _merge(x, mod, g, ysb, ylru, ygla, wg, wbr, wout, *, per_token, tok_per_seq, tm):
    n, d = x.shape
    row = lambda w: pl.BlockSpec((tm, w), lambda i: (i, 0))
    return pl.pallas_call(
        functools.partial(_merge_kernel, per_token=per_token),
        grid=(n // tm,),
        in_specs=[
            row(d),
            _mod_spec(per_token, tm, d, tok_per_seq),
            pl.BlockSpec((1, d), lambda i: (0, 0)),
            row(SB_W), row(LRU_W), row(GLA_VW),
            pl.BlockSpec((d, 3 * d), lambda i: (0, 0)),
            pl.BlockSpec((3, SB_W, d), lambda i: (0, 0, 0)),
            pl.BlockSpec((d, d), lambda i: (0, 0)),
        ],
        out_specs=row(d),
        out_shape=jax.ShapeDtypeStruct((n, d), F32),
        compiler_params=_params(("parallel",)),
    )(x, mod, g, ysb, ylru, ygla, wg, wbr, wout)


def _ffn_kernel(x_ref, mod_ref, g_ref, wgate_ref, wup_ref, wdown_ref, gfin_ref, o_ref, h_s, acc_s,
                *, per_token, final):
    f = pl.program_id(1)

    @pl.when(f == 0)
    def _():
        h_s[...] = _norm_mod(x_ref[...], g_ref[...], _mod_row(mod_ref, 3, per_token),
                             _mod_row(mod_ref, 4, per_token)).astype(BF16)
        acc_s[...] = jnp.zeros(acc_s.shape, F32)

    h = h_s[...]
    ff = _silu(jnp.dot(h, wgate_ref[...], preferred_element_type=F32)) * jnp.dot(
        h, wup_ref[...], preferred_element_type=F32)
    acc_s[...] += jnp.dot(ff.astype(BF16), wdown_ref[...], preferred_element_type=F32)

    @pl.when(f == pl.num_programs(1) - 1)
    def _():
        x2 = x_ref[...] + _mod_row(mod_ref, 5, per_token) * acc_s[...]
        if final:
            x2 = x2 * lax.rsqrt(jnp.mean(x2 * x2, axis=-1, keepdims=True) + EPS) * gfin_ref[...]
        o_ref[...] = x2


def _ffn(x, mod, g, wgate, wup, wdown, gfin, *, per_token, tok_per_seq, tm, final):
    n, d = x.shape
    dff = wgate.shape[1]
    tf = dff
    once = dict(pipeline_mode=pl.Buffered(1))
    mod_spec = _mod_spec(per_token, tm, d, tok_per_seq)
    return pl.pallas_call(
        functools.partial(_ffn_kernel, per_token=per_token, final=final),
        grid=(n // tm, dff // tf),
        in_specs=[
            pl.BlockSpec((tm, d), lambda i, f: (i, 0)),
            mod_spec,
            pl.BlockSpec((1, d), lambda i, f: (0, 0)),
            pl.BlockSpec((d, tf), lambda i, f: (0, f), **once),
            pl.BlockSpec((d, tf), lambda i, f: (0, f), **once),
            pl.BlockSpec((tf, d), lambda i, f: (f, 0), **once),
            pl.BlockSpec((1, d), lambda i, f: (0, 0)),
        ],
        out_specs=pl.BlockSpec((tm, d), lambda i, f: (i, 0)),
        out_shape=jax.ShapeDtypeStruct((n, d), F32),
        scratch_shapes=[pltpu.VMEM((tm, d), BF16), pltpu.VMEM((tm, d), F32)],
        compiler_params=_params(("parallel", "arbitrary")),
    )(x, mod, g, wgate, wup, wdown, gfin)


def _split_w_in(w_in_l):
    d = w_in_l.shape[0]
    sizes = (SB_W, SB_W, SB_W, LRU_W, LRU_W, GLA_KW, GLA_KW, GLA_VW, GLA_VW, GLA_RANK, d, d, d)
    offs = np.concatenate([[0], np.cumsum(sizes)])
    seg = lambda i: w_in_l[:, offs[i]:offs[i + 1]]
    glr = jnp.pad(seg(9), ((0, 0), (0, LANES - GLA_RANK)))
    w_mix = jnp.concatenate([seg(i) for i in range(9)] + [glr], axis=1).astype(BF16)
    w_gates = jnp.concatenate([seg(10), seg(11), seg(12)], axis=1).astype(BF16)
    return w_mix, w_gates


def _lru_gate_tiles(wa, wx):
    nb = wa.shape[0]
    z = jnp.zeros((LRU_BLK, LRU_BLK), F32)

    def pair(w, j):
        return jnp.concatenate([jnp.concatenate([w[2 * j], z], axis=1),
                                jnp.concatenate([z, w[2 * j + 1]], axis=1)], axis=0)

    tiles = [jnp.concatenate([pair(wa, j), pair(wx, j)], axis=1) for j in range(nb // 2)]
    return jnp.stack(tiles).astype(BF16)


def _pick_tile(n, pref):
    t = pref
    while n % t:
        t //= 2
    return t


def kernel(x_prompt, x_sample, c_prompt, c_sample, cache_k, cache_v, page_table, state_conv, state_lru, state_gla, w_ada, b_ada, norm_mix, w_in, sb_bias, conv_w, conv_b, lru_wa, lru_ba, lru_wx, lru_bx, lru_lambda, gla_w_alpha, gla_b_alpha, gla_norm, w_br_sb, w_br_lru, w_br_gla, w_out, norm_ffn, w_gate, w_up, w_down, norm_final):
    bp, seq, d = x_prompt.shape
    bs, n_new, _ = x_sample.shape
    depth = w_ada.shape[0]
    n_heads = SB_W // SB_DH
    n_pool = cache_k.shape[1]
    tsp = SAMPLE_PAD_T

    xp = x_prompt.reshape(bp * seq, d)
    xs = jnp.pad(x_sample, ((0, 0), (0, tsp - n_new), (0, 0))).reshape(bs * tsp, d)
    ck = jnp.transpose(cache_k, (0, 1, 3, 4, 2)).reshape(depth, n_pool, SB_W, PAGE)
    cv = jnp.transpose(cache_v, (0, 1, 3, 4, 2)).reshape(depth, n_pool, SB_W, PAGE)

    mod = _ada(jnp.concatenate([c_prompt, c_sample], axis=0), w_ada, b_ada)
    mod = mod.reshape(depth, bp + bs, 6, d)
    mod_p = mod[:, :bp]
    mod_s = jnp.transpose(jnp.repeat(mod[:, bp:], tsp, axis=1), (0, 2, 1, 3))

    tm_p = _pick_tile(seq, 512)
    tm_s = bs * tsp
    tc_p = _pick_tile(seq, 256)
    c_gla = _pick_tile(seq, 64)
    group = _pick_tile(page_table.shape[1], 16)

    zeros_conv = jnp.zeros((bp, CONV_TAPS - 1, LRU_W), F32)
    zeros_h = jnp.zeros((bp, 1, LRU_W), F32)
    zeros_s = jnp.zeros((bp, GLA_KW, GLA_DV), F32)
    row2 = lambda a: a.reshape(1, -1)

    outs = {k: [] for k in ("cvp", "hp", "sp", "ks", "vs", "cvs", "hs", "ss")}
    k_slab = v_slab = None
    for l in range(depth):
        w_mix, w_gates = _split_w_in(w_in[l])
        wg_lru = _lru_gate_tiles(lru_wa[l], lru_wx[l])
        wal = jnp.pad(gla_w_alpha[l], ((0, LANES - GLA_RANK), (0, 0))).astype(BF16)
        wbr = jnp.stack([w_br_sb[l], w_br_lru[l], w_br_gla[l]]).astype(BF16)
        wout = w_out[l].astype(BF16)
        wgate, wup, wdown = w_gate[l].astype(BF16), w_up[l].astype(BF16), w_down[l].astype(BF16)
        lru_args = (conv_w[l], row2(conv_b[l]), wg_lru, row2(lru_ba[l]), row2(lru_bx[l]), row2(lru_lambda[l]))
        gla_args = (wal, row2(gla_b_alpha[l]), row2(gla_norm[l]))
        final = l == depth - 1

        qb, k_slab, v_slab, kb, vb, lru_in, gla_in = _inproj(
            xp, mod_p[l], row2(norm_mix[l]), w_mix, per_token=False, tok_per_seq=seq, tm=tm_p,
            kv_slab=(l, depth, k_slab, v_slab))
        y_sb = _sb_prompt(qb, kb, vb, sb_bias[l], batch=bp, seq=seq)
        y_lru, cv1, h1 = _lru(lru_in, zeros_conv, zeros_h, *lru_args, batch=bp, tok_per_seq=seq,
                              tc=tc_p, n_valid=tc_p, out_dtype=BF16)
        y_gla, s1 = _gla(gla_in, zeros_s, *gla_args, batch=bp, tok_per_seq=seq, c=c_gla,
                         n_valid=c_gla, out_dtype=BF16)
        xp = _merge(xp, mod_p[l], row2(norm_mix[l]), y_sb, y_lru, y_gla, w_gates, wbr, wout,
                    per_token=False, tok_per_seq=seq, tm=tm_p)
        xp = _ffn(xp, mod_p[l], row2(norm_ffn[l]), wgate, wup, wdown, row2(norm_final),
                  per_token=False, tok_per_seq=seq, tm=tm_p, final=final)
        outs["cvp"].append(cv1)
        outs["hp"].append(h1.reshape(bp, LRU_W))
        outs["sp"].append(s1.reshape(bp, GLA_H, GLA_DK, GLA_DV))

        qb, k, v, _, _, lru_in, gla_in = _inproj(
            xs, mod_s[l], row2(norm_mix[l]), w_mix, per_token=True, tok_per_seq=tsp, tm=tm_s)
        q4 = qb.reshape(bs, tsp, n_heads, SB_DH)[:, :n_new]
        eye = jnp.eye(n_heads, dtype=BF16)
        qbd = (q4[:, :, None, :, :] * eye[None, None, :, :, None]).reshape(bs, n_new * n_heads, SB_W)
        bias_rows = jnp.broadcast_to(jnp.tile(sb_bias[l], n_new)[:, None], (n_new * n_heads, 2 * PAGE))
        y_sb = _sb_sample(qbd, k, v, bias_rows, ck, cv, page_table, l, n_new=n_new, group=group)
        y_lru, cv2, h2 = _lru(lru_in, state_conv[l], state_lru[l].reshape(bs, 1, LRU_W), *lru_args,
                              batch=bs, tok_per_seq=tsp, tc=tsp, n_valid=n_new, out_dtype=F32)
        gla_pad = jnp.pad(gla_in.reshape(bs, tsp, GLA_IN_W),
                          ((0, 0), (0, GLA_SAMPLE_C - tsp), (0, 0))).reshape(bs * GLA_SAMPLE_C, GLA_IN_W)
        y_gla, s2 = _gla(gla_pad, state_gla[l].reshape(bs, GLA_KW, GLA_DV), *gla_args, batch=bs,
                         tok_per_seq=GLA_SAMPLE_C, c=GLA_SAMPLE_C, n_valid=n_new, out_dtype=F32)
        y_gla = y_gla.reshape(bs, GLA_SAMPLE_C, GLA_VW)[:, :tsp].reshape(bs * tsp, GLA_VW)
        xs = _merge(xs, mod_s[l], row2(norm_mix[l]), y_sb, y_lru, y_gla, w_gates, wbr, wout,
                    per_token=True, tok_per_seq=tsp, tm=tm_s)
        xs = _ffn(xs, mod_s[l], row2(norm_ffn[l]), wgate, wup, wdown, row2(norm_final),
                  per_token=True, tok_per_seq=tsp, tm=tm_s, final=final)
        outs["ks"].append(k.reshape(bs, tsp, n_heads, SB_DH)[:, :n_new])
        outs["vs"].append(v.reshape(bs, tsp, n_heads, SB_DH)[:, :n_new])
        outs["cvs"].append(cv2)
        outs["hs"].append(h2.reshape(bs, LRU_W))
        outs["ss"].append(s2.reshape(bs, GLA_H, GLA_DK, GLA_DV))

    y_prompt = xp.reshape(bp, seq, d)
    y_sample = xs.reshape(bs, tsp, d)[:, :n_new]
    st = lambda name: jnp.stack(outs[name])
    rows = lambda slab: jnp.transpose(slab.reshape(depth, bp, n_heads, SB_DH, seq), (0, 1, 4, 2, 3))
    return (y_prompt, y_sample, rows(k_slab), rows(v_slab), st("cvp"), st("hp"), st("sp"),
            st("ks"), st("vs"), st("cvs"), st("hs"), st("ss"))
```

```python
import functools

import numpy as np
import jax
import jax.numpy as jnp
from jax import lax
from jax.experimental import pallas as pl
from jax.experimental.pallas import tpu as pltpu

F32 = jnp.float32
BF16 = jnp.bfloat16

SB_DH = 64
SB_W = 512
LRU_W = 512
LRU_BLK = 64
CONV_TAPS = 4
LRU_SCALE = 8.0
GLA_H = 4
GLA_DK = 64
GLA_DV = 128
GLA_KW = GLA_H * GLA_DK
GLA_VW = GLA_H * GLA_DV
GLA_RANK = 16
GLA_TAU = 16.0
EPS = 1e-6
PAGE = 128
SB_TILE = 256

LANES = 128
SUBLANES = 8
VMEM_LIMIT = 56 * 1024 * 1024

SAMPLE_PAD_T = 8
GLA_SAMPLE_C = 32
GLA_IN_W = GLA_KW * 2 + GLA_VW * 2 + LANES
MIX_W = 3 * SB_W + 2 * LRU_W + GLA_IN_W


def _params(sem):
    return pltpu.CompilerParams(dimension_semantics=sem, vmem_limit_bytes=VMEM_LIMIT)


def _softplus(x):
    return jnp.maximum(x, 0.0) + jnp.log1p(jnp.exp(-jnp.abs(x)))


def _log_sigmoid(x):
    return jnp.minimum(x, 0.0) - jnp.log1p(jnp.exp(-jnp.abs(x)))


def _silu(x):
    return x * jax.nn.sigmoid(x)


def _gelu_tanh(x):
    return 0.5 * x * (1.0 + jnp.tanh(0.7978845608028654 * (x + 0.044715 * (x * x * x))))


def _split_bf16(x):
    hi = x.astype(BF16)
    lo = (x - hi.astype(F32)).astype(BF16)
    return hi, lo


def _mod_row(mod_ref, idx, per_token):
    if per_token:
        return mod_ref[idx]
    return mod_ref[idx:idx + 1, :]


def _norm_mod(x, g, shift, scale):
    xn = x * lax.rsqrt(jnp.mean(x * x, axis=-1, keepdims=True) + EPS) * g
    return xn * (1.0 + scale) + shift


def _mod_spec(per_token, tm, d, tok_per_seq):
    if per_token:
        return pl.BlockSpec((6, tm, d), lambda i, *_: (0, i, 0))
    blocks_per_seq = tok_per_seq // tm
    return pl.BlockSpec((None, 6, d), lambda i, *_: (i // blocks_per_seq, 0, 0))


def _ada_kernel(c_ref, w_ref, b_ref, o_ref):
    c = c_ref[...]
    s = _silu(c).astype(BF16)
    o_ref[...] = jnp.dot(s, w_ref[...].astype(BF16), preferred_element_type=F32) + b_ref[...]


def _ada(c_all, w_ada, b_ada):
    depth, d, e = w_ada.shape
    n = c_all.shape[0]
    tn = 1536 if e % 1536 == 0 else e
    return pl.pallas_call(
        _ada_kernel,
        grid=(depth, e // tn),
        in_specs=[
            pl.BlockSpec((n, d), lambda l, j: (0, 0)),
            pl.BlockSpec((None, d, tn), lambda l, j: (l, 0, j)),
            pl.BlockSpec((None, 1, tn), lambda l, j: (l, 0, j)),
        ],
        out_specs=pl.BlockSpec((None, n, tn), lambda l, j: (l, 0, j)),
        out_shape=jax.ShapeDtypeStruct((depth, n, e), F32),
        compiler_params=_params(("parallel", "parallel")),
    )(c_all, w_ada, b_ada.reshape(depth, 1, e))


def _inproj_kernel(x_ref, mod_ref, g_ref, w_ref, *rest, per_token, kv_slab):
    q_ref, k_ref, v_ref, kb_ref, vb_ref, lru_ref, gla_ref = rest[-7:]
    h = _norm_mod(x_ref[...], g_ref[...], _mod_row(mod_ref, 0, per_token),
                  _mod_row(mod_ref, 1, per_token)).astype(BF16)

    def seg(a, b):
        return jnp.dot(h, w_ref[:, a:b], preferred_element_type=F32)

    q = seg(0, SB_W) * (SB_DH ** -0.5)
    k = seg(SB_W, 2 * SB_W)
    v = seg(2 * SB_W, 3 * SB_W)
    if kv_slab:
        perm, perm_t = rest[0][...], rest[1][...]
        q_ref[...] = q.T.astype(BF16)
        k_ref[...] = k.T
        v_t = v.T
        v_ref[...] = v_t
        kb = k.astype(BF16)
        vb = v.astype(BF16)
        for t in range(k.shape[0] // SB_TILE):
            sl = slice(t * SB_TILE, (t + 1) * SB_TILE)
            kb_ref[sl, :] = jnp.dot(perm, kb[sl, :], preferred_element_type=F32).astype(BF16)
            vb_ref[t] = lax.dot_general(vb[sl, :], perm_t, (((0,), (0,)), ((), ())),
                                        preferred_element_type=F32).astype(BF16)
    else:
        q_ref[...] = q.astype(BF16)
        k_ref[...] = k
        kb_ref[...] = k.astype(BF16)
        v_ref[...] = v
        vb_ref[...] = v.astype(BF16)
    o = 3 * SB_W
    lru_ref[...] = seg(o, o + 2 * LRU_W)
    o += 2 * LRU_W
    gla_ref[...] = seg(o, o + GLA_IN_W)


def _key_permutation():
    groups = SB_TILE // SUBLANES
    p = np.zeros((SB_TILE, SB_TILE), np.float32)
    for r in range(groups):
        for sub in range(SUBLANES):
            p[SUBLANES * r + sub, groups * sub + r] = 1.0
    return jnp.asarray(p, dtype=BF16)


def _inproj(x, mod, g, w_mix, *, per_token, tok_per_seq, tm, kv_slab=None):
    n, d = x.shape
    row = lambda w: pl.BlockSpec((tm, w), lambda i: (i, 0))
    outs = [(SB_W, BF16), (SB_W, F32), (SB_W, F32), (SB_W, BF16), (SB_W, BF16),
            (2 * LRU_W, F32), (GLA_IN_W, F32)]
    out_specs = [row(w) for w, _ in outs]
    out_shape = [jax.ShapeDtypeStruct((n, w), dt) for w, dt in outs]
    extra_in, extra_specs, aliases = [], [], {}
    if kv_slab is not None:
        layer, depth, k_slab, v_slab = kv_slab
        bps = tok_per_seq // tm
        batch = n // tok_per_seq
        slab_spec = pl.BlockSpec((None, None, SB_W, tm), lambda i: (layer, i // bps, 0, i % bps))
        slab_shape = jax.ShapeDtypeStruct((depth, batch, SB_W, tok_per_seq), F32)
        out_specs[1] = out_specs[2] = slab_spec
        out_shape[1] = out_shape[2] = slab_shape
        out_specs[0] = pl.BlockSpec((None, SB_W, tm), lambda i: (i // bps, 0, i % bps))
        out_shape[0] = jax.ShapeDtypeStruct((batch, SB_W, tok_per_seq), BF16)
        tpb = tm // SB_TILE
        out_specs[4] = pl.BlockSpec((None, tpb, SB_W, SB_TILE), lambda i: (i // bps, i % bps, 0, 0))
        out_shape[4] = jax.ShapeDtypeStruct((batch, tok_per_seq // SB_TILE, SB_W, SB_TILE), BF16)
        perm = _key_permutation()
        extra_in = [perm, perm.T]
        extra_specs = [pl.BlockSpec((SB_TILE, SB_TILE), lambda i: (0, 0))] * 2
        if k_slab is not None:
            extra_in += [k_slab, v_slab]
            extra_specs += [pl.BlockSpec(memory_space=pl.ANY)] * 2
            aliases = {6: 1, 7: 2}
    return pl.pallas_call(
        functools.partial(_inproj_kernel, per_token=per_token, kv_slab=kv_slab is not None),
        grid=(n // tm,),
        in_specs=[
            row(d),
            _mod_spec(per_token, tm, d, tok_per_seq),
            pl.BlockSpec((1, d), lambda i: (0, 0)),
            pl.BlockSpec((d, MIX_W), lambda i: (0, 0)),
        ] + extra_specs,
        out_specs=out_specs,
        out_shape=out_shape,
        input_output_aliases=aliases,
        compiler_params=_params(("parallel",)),
    )(x, mod, g, w_mix, *extra_in)


def _sb_logs(z):
    t = jnp.log(1.0 + jnp.exp(-jnp.abs(z)))
    m = jnp.minimum(z, 0.0)
    return m - t, (m - z) - t


def _sb_logs_neg(z):
    neg_log_keep = jnp.maximum(z, 0.0) + jnp.log(1.0 + jnp.exp(-jnp.abs(z)))
    return z - neg_log_keep, neg_log_keep


def _suffix_lhs(log_keep, m2):
    if m2.shape[0] == log_keep.shape[1]:
        return log_keep.astype(BF16)
    return jnp.concatenate(_split_bf16(log_keep), axis=1)


def _sb_tile(qh, kblk, vblk, bias, m2, carry, acc, valid):
    z = lax.dot_general(qh, kblk, (((1,), (1,)), ((), ())), preferred_element_type=F32) + bias
    log_beta, log_keep = _sb_logs(z)
    if valid is not None:
        log_keep = jnp.where(valid, log_keep, 0.0)
    sfx = jnp.dot(_suffix_lhs(log_keep, m2), m2, preferred_element_type=F32)
    a = jnp.exp(log_beta + sfx + carry)
    if valid is not None:
        a = jnp.where(valid, a, 0.0)
    carry = carry + jnp.sum(log_keep, axis=1, keepdims=True)
    acc = acc + jnp.dot(a.astype(BF16), vblk, preferred_element_type=F32)
    return carry, acc


def _sb_tiles_keys_major(q2t, k_rows, vt_cols, bias_row, top_valid):
    n_tiles = k_rows.shape[0] // SB_TILE
    groups = SB_TILE // SUBLANES
    width = q2t.shape[1]
    zt = jnp.dot(k_rows, q2t, preferred_element_type=F32) + bias_row
    log_beta, log_keep = _sb_logs_neg(zt)
    sub = lax.broadcasted_iota(jnp.int32, (SUBLANES, width), 0)
    carry = jnp.zeros((SUBLANES, width), F32)
    a_tiles = [None] * n_tiles
    for t in reversed(range(n_tiles)):
        top = t == n_tiles - 1
        rows = [slice(t * SB_TILE + SUBLANES * r, t * SB_TILE + SUBLANES * (r + 1)) for r in range(groups)]
        masks = [top_valid[SUBLANES * r:SUBLANES * (r + 1)] for r in range(groups)] if top else None
        below = [None] * groups
        run = jnp.zeros((SUBLANES, width), F32)
        for r in reversed(range(groups)):
            below[r] = run
            lk = log_keep[rows[r]]
            run = run + (jnp.where(masks[r], lk, 0.0) if top else lk)
        inc = run
        for d in (1, 2, 4):
            inc = inc + jnp.where(sub < SUBLANES - d, pltpu.roll(inc, SUBLANES - d, axis=0), 0.0)
        later = (inc - run) + carry
        carry = carry + jnp.broadcast_to(inc[0:1], (SUBLANES, width))
        a_rows = [jnp.exp(log_beta[rows[r]] - below[r] - later) for r in range(groups)]
        if top:
            a_rows = [jnp.where(masks[r], a_rows[r], 0.0) for r in range(groups)]
        a_tiles[t] = jnp.concatenate(a_rows, axis=0)
    a = a_tiles[0] if n_tiles == 1 else jnp.concatenate(a_tiles, axis=0)
    return jnp.dot(vt_cols, a.astype(BF16), preferred_element_type=F32)


def _sb_prompt_kernel(bias_ref, qt_ref, k_ref, vt_ref, o_ref, *, tq, nq):
    hp = pl.program_id(1)
    qi = pl.program_id(2)
    qt = qt_ref[...].astype(F32)
    feat = lax.broadcasted_iota(jnp.int32, (LANES, tq), 0)
    q2t = jnp.concatenate([jnp.where(feat < SB_DH, qt, 0.0), jnp.where(feat >= SB_DH, qt, 0.0)],
                          axis=1).astype(BF16)
    lane1 = lax.broadcasted_iota(jnp.int32, (1, 2 * tq), 1)
    bias_row = jnp.where(lane1 < tq, bias_ref[2 * hp], bias_ref[2 * hp + 1])
    row = lax.broadcasted_iota(jnp.int32, (tq, 2 * tq), 0)
    lane = lax.broadcasted_iota(jnp.int32, (tq, 2 * tq), 1)
    key = (row & (SUBLANES - 1)) * (tq // SUBLANES) + (row >> 3)
    diag_valid = key < (lane & (tq - 1))

    def block(n):
        vt = vt_ref[0] if n == 1 else jnp.concatenate([vt_ref[t] for t in range(n)], axis=1)
        return _sb_tiles_keys_major(q2t, k_ref[0:n * tq, :], vt, bias_row, diag_valid)

    acc = lax.switch(qi, [functools.partial(block, n) for n in range(1, nq + 1)])
    out_t = jnp.concatenate([acc[:SB_DH, :tq], acc[SB_DH:, tq:]], axis=0)
    o_ref[...] = out_t.T.astype(o_ref.dtype)


def _suffix_matrix(t, parts):
    m = (np.arange(t)[:, None] > np.arange(t)[None, :]).astype(np.float32)
    return jnp.asarray(np.concatenate([m] * parts, axis=0), dtype=BF16)


def _sb_prompt(qt, kperm, vt_tiles, bias, *, batch, seq):
    tq = SB_TILE
    nq = seq // tq
    assert nq <= 16, "one unrolled body per query block: sequence too long for this kernel"
    return pl.pallas_call(
        functools.partial(_sb_prompt_kernel, tq=tq, nq=nq),
        grid=(batch, SB_W // LANES, nq),
        in_specs=[
            pl.BlockSpec(memory_space=pltpu.SMEM),
            pl.BlockSpec((None, LANES, tq), lambda b, h, i: (b, h, i)),
            pl.BlockSpec((seq, LANES), lambda b, h, i: (b, h)),
            pl.BlockSpec((None, nq, LANES, tq), lambda b, h, i: (b, 0, h, 0)),
        ],
        out_specs=pl.BlockSpec((tq, LANES), lambda b, h, i: (b * nq + i, h)),
        out_shape=jax.ShapeDtypeStruct((batch * seq, SB_W), BF16),
        compiler_params=_params(("parallel", "parallel", "arbitrary")),
    )(bias, qt, kperm, vt_tiles)


def _sb_sample_kernel(pt_ref, qbd_ref, kn_ref, vn_ref, bias_ref, m2_ref, *refs, group, n_new, n_heads):
    del pt_ref
    kp = refs[:group]
    vp = refs[group:2 * group]
    o_ref = refs[2 * group]
    carry_ref, acc_ref = refs[2 * group + 1:]
    p = pl.program_id(1)
    qbd = qbd_ref[...]
    bias = bias_ref[...]
    m2 = m2_ref[...]
    rows = n_new * n_heads
    width = 2 * PAGE
    n_pair = group // 2

    @pl.when(p == 0)
    def _():
        pad = jnp.zeros((width - SAMPLE_PAD_T, SB_W), F32)
        kn = jnp.concatenate([kn_ref[...], pad], axis=0).astype(BF16)
        vn = jnp.concatenate([vn_ref[...], pad], axis=0).astype(BF16)
        row = lax.broadcasted_iota(jnp.int32, (rows, width), 0)
        col = lax.broadcasted_iota(jnp.int32, (rows, width), 1)
        valid = col < row // n_heads
        carry, acc = _sb_tile(qbd, kn, vn, bias, m2, jnp.zeros((rows, 1), F32),
                              jnp.zeros((rows, SB_W), F32), valid)
        carry_ref[...] = carry
        acc_ref[...] = acc

    def pair(page_refs, j):
        low = page_refs[group - 1 - 2 * j][...]
        high = page_refs[group - 2 - 2 * j][...]
        return jnp.concatenate([low, high], axis=1).astype(BF16)

    logs = [_sb_logs(jnp.dot(qbd, pair(kp, j), preferred_element_type=F32) + bias) for j in range(n_pair)]
    split = [jnp.concatenate(_split_bf16(lk), axis=1) for _, lk in logs]
    sfx = jnp.dot(jnp.concatenate(split, axis=0), m2, preferred_element_type=F32)
    carry = carry_ref[...]
    carries = [None] * n_pair
    for j in reversed(range(n_pair)):
        carries[j] = carry
        carry = carry + jnp.sum(logs[j][1], axis=1, keepdims=True)
    acc = acc_ref[...]
    for j in range(n_pair):
        a = jnp.exp(logs[j][0] + sfx[j * rows:(j + 1) * rows] + carries[j]).astype(BF16)
        acc = acc + lax.dot_general(a, pair(vp, j), (((1,), (1,)), ((), ())), preferred_element_type=F32)
    carry_ref[...] = carry
    acc_ref[...] = acc

    @pl.when(p == pl.num_programs(1) - 1)
    def _():
        row = lax.broadcasted_iota(jnp.int32, (rows, SB_W), 0)
        col = lax.broadcasted_iota(jnp.int32, (rows, SB_W), 1)
        own = jnp.where(col // SB_DH == row % n_heads, acc, 0.0)
        o_ref[...] = jnp.zeros(o_ref.shape, o_ref.dtype)
        for qq in range(n_new):
            o_ref[qq:qq + 1, :] = jnp.sum(own[qq * n_heads:(qq + 1) * n_heads, :], axis=0, keepdims=True)


def _sb_sample(qbd, k_new, v_new, bias_rows, cache_k, cache_v, page_table, layer, *, n_new, group):
    bs, n_pages = page_table.shape
    n_heads = SB_W // SB_DH
    rows = n_new * n_heads
    steps = n_pages // group

    def page_spec(i):
        def index(b, p, pt):
            return (layer, pt[b, n_pages - 1 - (p * group + i)], 0, 0)
        return pl.BlockSpec((None, None, SB_W, PAGE), index)

    grid_spec = pltpu.PrefetchScalarGridSpec(
        num_scalar_prefetch=1,
        grid=(bs, steps),
        in_specs=[
            pl.BlockSpec((None, rows, SB_W), lambda b, p, pt: (b, 0, 0)),
            pl.BlockSpec((SAMPLE_PAD_T, SB_W), lambda b, p, pt: (b, 0)),
            pl.BlockSpec((SAMPLE_PAD_T, SB_W), lambda b, p, pt: (b, 0)),
            pl.BlockSpec((rows, 2 * PAGE), lambda b, p, pt: (0, 0)),
            pl.BlockSpec((4 * PAGE, 2 * PAGE), lambda b, p, pt: (0, 0)),
        ] + [page_spec(i) for i in range(group)] * 2,
        out_specs=pl.BlockSpec((SAMPLE_PAD_T, SB_W), lambda b, p, pt: (b, 0)),
        scratch_shapes=[pltpu.VMEM((rows, 1), F32), pltpu.VMEM((rows, SB_W), F32)],
    )
    return pl.pallas_call(
        functools.partial(_sb_sample_kernel, group=group, n_new=n_new, n_heads=n_heads),
        grid_spec=grid_spec,
        out_shape=jax.ShapeDtypeStruct((bs * SAMPLE_PAD_T, SB_W), F32),
        compiler_params=_params(("parallel", "arbitrary")),
    )(page_table, qbd, k_new, v_new, bias_rows, _suffix_matrix(2 * PAGE, 2),
      *([cache_k] * group), *([cache_v] * group))


def _lru_kernel(x_ref, conv0_ref, h0_ref, cw_ref, cb_ref, wg_ref, ba_ref, bx_ref, lam_ref,
                y_ref, conv_ref, hl_ref, xpad, a_s, b_s, hs_s, hcar, *, tc, n_valid):
    ci = pl.program_id(1)
    head = SUBLANES
    tail = CONV_TAPS - 1

    @pl.when(ci == 0)
    def _():
        xpad[head - tail:head, :] = conv0_ref[...]
        hcar[...] = h0_ref[...]

    xpad[head:head + tc, :] = x_ref[:, 0:LRU_W]
    xc = cb_ref[...] + xpad[head - tail:head - tail + tc, :] * cw_ref[0:1, :]
    for w in range(1, CONV_TAPS):
        xc = xc + xpad[head - tail + w:head - tail + w + tc, :] * cw_ref[w:w + 1, :]
    xcb = xc.astype(BF16)
    neg_sp = -LRU_SCALE * _softplus(-lam_ref[...])
    for j in range(LRU_W // LANES):
        sl = slice(j * LANES, (j + 1) * LANES)
        g = jnp.dot(xcb[:, sl], wg_ref[j], preferred_element_type=F32)
        r = jax.nn.sigmoid(g[:, :LANES] + ba_ref[:, sl])
        gate_in = jax.nn.sigmoid(g[:, LANES:] + bx_ref[:, sl])
        log_a = neg_sp[:, sl] * r
        a = jnp.exp(log_a)
        mult = jnp.sqrt(-jnp.tanh(log_a) * (a * a + 1.0))
        a_s[:, sl] = a
        b_s[:, sl] = mult * (gate_in * xc[:, sl])

    rowi = lax.broadcasted_iota(jnp.int32, (SUBLANES, LRU_W), 0)

    def tile_scan(g, h):
        r0 = pl.multiple_of(g * SUBLANES, SUBLANES)
        a = a_s[pl.ds(r0, SUBLANES), :]
        b = b_s[pl.ds(r0, SUBLANES), :]
        for d in (1, 2, 4):
            keep = rowi >= d
            b = jnp.where(keep, a * pltpu.roll(b, d, axis=0) + b, b)
            a = jnp.where(keep, a * pltpu.roll(a, d, axis=0), a)
        hs = a * h + b
        hs_s[pl.ds(r0, SUBLANES), :] = hs
        return hs[SUBLANES - 1:SUBLANES, :]

    hcar[...] = lax.fori_loop(0, tc // SUBLANES, tile_scan, hcar[...])
    y_ref[...] = (hs_s[...] * _gelu_tanh(x_ref[:, LRU_W:2 * LRU_W])).astype(y_ref.dtype)

    new_tail = xpad[head + n_valid - tail:head + n_valid, :]
    conv_ref[...] = new_tail
    hl_ref[...] = hs_s[n_valid - 1:n_valid, :]
    xpad[head - tail:head, :] = new_tail


def _lru(lru_in, conv0, h0, cw, cb, wg, ba, bx, lam, *, batch, tok_per_seq, tc, n_valid, out_dtype):
    n = lru_in.shape[0]
    nc = tok_per_seq // tc
    vec = pl.BlockSpec((1, LRU_W), lambda b, c: (0, 0))
    tail = CONV_TAPS - 1
    return pl.pallas_call(
        functools.partial(_lru_kernel, tc=tc, n_valid=n_valid),
        grid=(batch, nc),
        in_specs=[
            pl.BlockSpec((tc, 2 * LRU_W), lambda b, c: (b * nc + c, 0)),
            pl.BlockSpec((None, tail, LRU_W), lambda b, c: (b, 0, 0)),
            pl.BlockSpec((None, 1, LRU_W), lambda b, c: (b, 0, 0)),
            pl.BlockSpec((CONV_TAPS, LRU_W), lambda b, c: (0, 0)),
            vec,
            pl.BlockSpec((LRU_W // LANES, LANES, 2 * LANES), lambda b, c: (0, 0, 0)),
            vec, vec, vec,
        ],
        out_specs=[
            pl.BlockSpec((tc, LRU_W), lambda b, c: (b * nc + c, 0)),
            pl.BlockSpec((None, tail, LRU_W), lambda b, c: (b, 0, 0)),
            pl.BlockSpec((None, 1, LRU_W), lambda b, c: (b, 0, 0)),
        ],
        out_shape=[
            jax.ShapeDtypeStruct((n, LRU_W), out_dtype),
            jax.ShapeDtypeStruct((batch, tail, LRU_W), F32),
            jax.ShapeDtypeStruct((batch, 1, LRU_W), F32),
        ],
        scratch_shapes=[
            pltpu.VMEM((SUBLANES + tc, LRU_W), F32),
            pltpu.VMEM((tc, LRU_W), F32),
            pltpu.VMEM((tc, LRU_W), F32),
            pltpu.VMEM((tc, LRU_W), F32),
            pltpu.VMEM((1, LRU_W), F32),
        ],
        compiler_params=_params(("parallel", "arbitrary")),
    )(lru_in, conv0, h0, cw, cb, wg, ba, bx, lam)


def _gla_kernel(x_ref, s0_ref, wal_ref, bal_ref, gn_ref, tri_ref, y_ref, sout_ref, st_ref, *, c, n_valid, nseq):
    ci = pl.program_id(1)
    seqs = range(nseq)
    nt = (((1,), (1,)), ((), ()))

    @pl.when(ci == 0)
    def _():
        for s in seqs:
            st_ref[s] = s0_ref[s].T

    xs = [x_ref[s] for s in seqs]
    o_v = 2 * GLA_KW
    o_go = o_v + GLA_VW
    o_lr = o_go + GLA_VW
    glr = jnp.concatenate([x[:, o_lr:] for x in xs], axis=0).astype(BF16)
    xl = jnp.dot(glr, wal_ref[...], preferred_element_type=F32) + bal_ref[...]
    la = _log_sigmoid(xl) * (1.0 / GLA_TAU)
    ks = [x[:, GLA_KW:o_v] for x in xs]
    las = [la[s * c:(s + 1) * c] for s in seqs]
    if n_valid < c:
        rowi = lax.broadcasted_iota(jnp.int32, (c, GLA_KW), 0)
        las = [jnp.where(rowi < n_valid, l, 0.0) for l in las]
        ks = [jnp.where(rowi < n_valid, k, 0.0) for k in ks]
    parts = [_split_bf16(l) for l in las]
    tri = tri_ref[...]
    g_all = (jnp.dot(tri, jnp.concatenate([p[0] for p in parts], axis=1), preferred_element_type=F32)
             + jnp.dot(tri, jnp.concatenate([p[1] for p in parts], axis=1), preferred_element_type=F32))

    lane_k = lax.broadcasted_iota(jnp.int32, (1, GLA_KW), 1) // GLA_DK
    row = lax.broadcasted_iota(jnp.int32, (c, GLA_H * c), 0)
    col = lax.broadcasted_iota(jnp.int32, (c, GLA_H * c), 1)
    causal = col % c <= row
    col_h = col // c

    def by_head(a):
        return jnp.concatenate([jnp.where(lane_k == h, a, 0.0) for h in range(GLA_H)], axis=0)

    q_ins, q_rels, kbd_rels, kbd_decs, decays = [], [], [], [], []
    for s in seqs:
        g = g_all[:, s * GLA_KW:(s + 1) * GLA_KW]
        g_last = g[c - 1:c, :]
        g_mid = g[c // 2:c // 2 + 1, :]
        q = xs[s][:, 0:GLA_KW] * (GLA_DK ** -0.5)
        q_ins.append(q * jnp.exp(g))
        q_rels.append((q * jnp.exp(g - g_mid)).astype(BF16))
        kbd_rels.append(by_head(ks[s] * jnp.exp(g_mid - g)).astype(BF16))
        kbd_decs.append(by_head(ks[s] * jnp.exp(g_last - g)).astype(BF16))
        decays.append(jnp.exp(g_last))

    atts = [lax.dot_general(q_rels[s], kbd_rels[s], nt, preferred_element_type=F32) for s in seqs]
    v_ts, w_ts, lhss = [], [], []
    for s in seqs:
        v = xs[s][:, o_v:o_go]
        v_t = jnp.concatenate([v[:, h * GLA_DV:(h + 1) * GLA_DV] for h in range(GLA_H)], axis=0).T
        v_ts.append(v_t)
        w_ts.append(jnp.concatenate([st_ref[s], v_t], axis=1).astype(BF16))
        att = jnp.where(causal, atts[s], 0.0)
        lhss.append(jnp.concatenate(
            [jnp.concatenate([jnp.where(lane_k == h, q_ins[s], 0.0), jnp.where(col_h == h, att, 0.0)], axis=1)
             for h in range(GLA_H)], axis=0).astype(BF16))
    ress = [lax.dot_general(lhss[s], w_ts[s], nt, preferred_element_type=F32) for s in seqs]
    upds = [jnp.dot(v_ts[s].astype(BF16), kbd_decs[s], preferred_element_type=F32) for s in seqs]

    for s in seqs:
        st_new = st_ref[s] * decays[s] + upds[s]
        st_ref[s] = st_new
        outs = []
        for h in range(GLA_H):
            oh = ress[s][h * c:(h + 1) * c, :]
            oh = oh * lax.rsqrt(jnp.mean(oh * oh, axis=-1, keepdims=True) + EPS)
            outs.append(oh * gn_ref[:, h * GLA_DV:(h + 1) * GLA_DV])
        go = xs[s][:, o_go:o_lr]
        y_ref[s] = (jnp.concatenate(outs, axis=1) * _silu(go)).astype(y_ref.dtype)

    @pl.when(ci == pl.num_programs(1) - 1)
    def _():
        for s in seqs:
            sout_ref[s] = st_ref[s].T


def _gla(gla_in, s0, wal, bal, gn, *, batch, tok_per_seq, c, n_valid, out_dtype):
    nc = tok_per_seq // c
    nseq = _pick_tile(batch, 4)
    tri = jnp.asarray((np.arange(c)[:, None] >= np.arange(c)[None, :]).astype(np.float32), dtype=BF16)
    state = pl.BlockSpec((nseq, GLA_KW, GLA_DV), lambda b, i: (b, 0, 0))
    y, s_out = pl.pallas_call(
        functools.partial(_gla_kernel, c=c, n_valid=n_valid, nseq=nseq),
        grid=(batch // nseq, nc),
        in_specs=[
            pl.BlockSpec((nseq, c, GLA_IN_W), lambda b, i: (b, i, 0)),
            state,
            pl.BlockSpec((LANES, GLA_KW), lambda b, i: (0, 0)),
            pl.BlockSpec((1, GLA_KW), lambda b, i: (0, 0)),
            pl.BlockSpec((1, GLA_VW), lambda b, i: (0, 0)),
            pl.BlockSpec((c, c), lambda b, i: (0, 0)),
        ],
        out_specs=[pl.BlockSpec((nseq, c, GLA_VW), lambda b, i: (b, i, 0)), state],
        out_shape=[
            jax.ShapeDtypeStruct((batch, tok_per_seq, GLA_VW), out_dtype),
            jax.ShapeDtypeStruct((batch, GLA_KW, GLA_DV), F32),
        ],
        scratch_shapes=[pltpu.VMEM((nseq, GLA_DV, GLA_KW), F32)],
        compiler_params=_params(("parallel", "arbitrary")),
    )(gla_in.reshape(batch, tok_per_seq, GLA_IN_W), s0, wal, bal, gn, tri)
    return y.reshape(batch * tok_per_seq, GLA_VW), s_out


def _merge_kernel(x_ref, mod_ref, g_ref, ysb_ref, ylru_ref, ygla_ref, wg_ref, wbr_ref, wout_ref,
                  o_ref, *, per_token):
    x = x_ref[...]
    d = x.shape[1]
    h = _norm_mod(x, g_ref[...], _mod_row(mod_ref, 0, per_token),
                  _mod_row(mod_ref, 1, per_token)).astype(BF16)
    merged = None
    for b, y_ref in enumerate((ysb_ref, ylru_ref, ygla_ref)):
        gate = jax.nn.sigmoid(jnp.dot(h, wg_ref[:, b * d:(b + 1) * d], preferred_element_type=F32))
        br = jnp.dot(y_ref[...].astype(BF16), wbr_ref[b], preferred_element_type=F32)
        merged = gate * br if merged is None else merged + gate * br
    out = jnp.dot(merged.astype(BF16), wout_ref[...], preferred_element_type=F32)
    o_ref[...] = x + _mod_row(mod_ref, 2, per_token) * out


def _merge(x, mod, g, ysb, ylru, ygla, wg, wbr, wout, *, per_token, tok_per_seq, tm):
    n, d = x.shape
    row = lambda w: pl.BlockSpec((tm, w), lambda i: (i, 0))
    return pl.pallas_call(
        functools.partial(_merge_kernel, per_token=per_token),
        grid=(n // tm,),
        in_specs=[
            row(d),
            _mod_spec(per_token, tm, d, tok_per_seq),
            pl.BlockSpec((1, d), lambda i: (0, 0)),
            row(SB_W), row(LRU_W), row(GLA_VW),
            pl.BlockSpec((d, 3 * d), lambda i: (0, 0)),
            pl.BlockSpec((3, SB_W, d), lambda i: (0, 0, 0)),
            pl.BlockSpec((d, d), lambda i: (0, 0)),
        ],
        out_specs=row(d),
        out_shape=jax.ShapeDtypeStruct((n, d), F32),
        compiler_params=_params(("parallel",)),
    )(x, mod, g, ysb, ylru, ygla, wg, wbr, wout)


def _ffn_kernel(x_ref, mod_ref, g_ref, wgate_ref, wup_ref, wdown_ref, gfin_ref, o_ref, h_s, acc_s,
                *, per_token, final):
    f = pl.program_id(1)

    @pl.when(f == 0)
    def _():
        h_s[...] = _norm_mod(x_ref[...], g_ref[...], _mod_row(mod_ref, 3, per_token),
                             _mod_row(mod_ref, 4, per_token)).astype(BF16)
        acc_s[...] = jnp.zeros(acc_s.shape, F32)

    h = h_s[...]
    ff = _silu(jnp.dot(h, wgate_ref[...], preferred_element_type=F32)) * jnp.dot(
        h, wup_ref[...], preferred_element_type=F32)
    acc_s[...] += jnp.dot(ff.astype(BF16), wdown_ref[...], preferred_element_type=F32)

    @pl.when(f == pl.num_programs(1) - 1)
    def _():
        x2 = x_ref[...] + _mod_row(mod_ref, 5, per_token) * acc_s[...]
        if final:
            x2 = x2 * lax.rsqrt(jnp.mean(x2 * x2, axis=-1, keepdims=True) + EPS) * gfin_ref[...]
        o_ref[...] = x2


def _ffn(x, mod, g, wgate, wup, wdown, gfin, *, per_token, tok_per_seq, tm, final):
    n, d = x.shape
    dff = wgate.shape[1]
    tf = dff
    once = dict(pipeline_mode=pl.Buffered(1))
    mod_spec = _mod_spec(per_token, tm, d, tok_per_seq)
    return pl.pallas_call(
        functools.partial(_ffn_kernel, per_token=per_token, final=final),
        grid=(n // tm, dff // tf),
        in_specs=[
            pl.BlockSpec((tm, d), lambda i, f: (i, 0)),
            mod_spec,
            pl.BlockSpec((1, d), lambda i, f: (0, 0)),
            pl.BlockSpec((d, tf), lambda i, f: (0, f), **once),
            pl.BlockSpec((d, tf), lambda i, f: (0, f), **once),
            pl.BlockSpec((tf, d), lambda i, f: (f, 0), **once),
            pl.BlockSpec((1, d), lambda i, f: (0, 0)),
        ],
        out_specs=pl.BlockSpec((tm, d), lambda i, f: (i, 0)),
        out_shape=jax.ShapeDtypeStruct((n, d), F32),
        scratch_shapes=[pltpu.VMEM((tm, d), BF16), pltpu.VMEM((tm, d), F32)],
        compiler_params=_params(("parallel", "arbitrary")),
    )(x, mod, g, wgate, wup, wdown, gfin)


def _split_w_in(w_in_l):
    d = w_in_l.shape[0]
    sizes = (SB_W, SB_W, SB_W, LRU_W, LRU_W, GLA_KW, GLA_KW, GLA_VW, GLA_VW, GLA_RANK, d, d, d)
    offs = np.concatenate([[0], np.cumsum(sizes)])
    seg = lambda i: w_in_l[:, offs[i]:offs[i + 1]]
    glr = jnp.pad(seg(9), ((0, 0), (0, LANES - GLA_RANK)))
    w_mix = jnp.concatenate([seg(i) for i in range(9)] + [glr], axis=1).astype(BF16)
    w_gates = jnp.concatenate([seg(10), seg(11), seg(12)], axis=1).astype(BF16)
    return w_mix, w_gates


def _lru_gate_tiles(wa, wx):
    nb = wa.shape[0]
    z = jnp.zeros((LRU_BLK, LRU_BLK), F32)

    def pair(w, j):
        return jnp.concatenate([jnp.concatenate([w[2 * j], z], axis=1),
                                jnp.concatenate([z, w[2 * j + 1]], axis=1)], axis=0)

    tiles = [jnp.concatenate([pair(wa, j), pair(wx, j)], axis=1) for j in range(nb // 2)]
    return jnp.stack(tiles).astype(BF16)


def _pick_tile(n, pref):
    t = pref
    while n % t:
        t //= 2
    return t


def kernel(x_prompt, x_sample, c_prompt, c_sample, cache_k, cache_v, page_table, state_conv, state_lru, state_gla, w_ada, b_ada, norm_mix, w_in, sb_bias, conv_w, conv_b, lru_wa, lru_ba, lru_wx, lru_bx, lru_lambda, gla_w_alpha, gla_b_alpha, gla_norm, w_br_sb, w_br_lru, w_br_gla, w_out, norm_ffn, w_gate, w_up, w_down, norm_final):
    bp, seq, d = x_prompt.shape
    bs, n_new, _ = x_sample.shape
    depth = w_ada.shape[0]
    n_heads = SB_W // SB_DH
    n_pool = cache_k.shape[1]
    tsp = SAMPLE_PAD_T

    xp = x_prompt.reshape(bp * seq, d)
    xs = jnp.pad(x_sample, ((0, 0), (0, tsp - n_new), (0, 0))).reshape(bs * tsp, d)
    ck = jnp.transpose(cache_k, (0, 1, 3, 4, 2)).reshape(depth, n_pool, SB_W, PAGE)
    cv = jnp.transpose(cache_v, (0, 1, 3, 4, 2)).reshape(depth, n_pool, SB_W, PAGE)

    mod = _ada(jnp.concatenate([c_prompt, c_sample], axis=0), w_ada, b_ada)
    mod = mod.reshape(depth, bp + bs, 6, d)
    mod_p = mod[:, :bp]
    mod_s = jnp.transpose(jnp.repeat(mod[:, bp:], tsp, axis=1), (0, 2, 1, 3))

    tm_p = _pick_tile(seq, 512)
    tm_s = bs * tsp
    tc_p = _pick_tile(seq, 256)
    c_gla = _pick_tile(seq, 64)
    group = _pick_tile(page_table.shape[1], 16)

    zeros_conv = jnp.zeros((bp, CONV_TAPS - 1, LRU_W), F32)
    zeros_h = jnp.zeros((bp, 1, LRU_W), F32)
    zeros_s = jnp.zeros((bp, GLA_KW, GLA_DV), F32)
    row2 = lambda a: a.reshape(1, -1)

    outs = {k: [] for k in ("cvp", "hp", "sp", "ks", "vs", "cvs", "hs", "ss")}
    k_slab = v_slab = None
    for l in range(depth):
        w_mix, w_gates = _split_w_in(w_in[l])
        wg_lru = _lru_gate_tiles(lru_wa[l], lru_wx[l])
        wal = jnp.pad(gla_w_alpha[l], ((0, LANES - GLA_RANK), (0, 0))).astype(BF16)
        wbr = jnp.stack([w_br_sb[l], w_br_lru[l], w_br_gla[l]]).astype(BF16)
        wout = w_out[l].astype(BF16)
        wgate, wup, wdown = w_gate[l].astype(BF16), w_up[l].astype(BF16), w_down[l].astype(BF16)
        lru_args = (conv_w[l], row2(conv_b[l]), wg_lru, row2(lru_ba[l]), row2(lru_bx[l]), row2(lru_lambda[l]))
        gla_args = (wal, row2(gla_b_alpha[l]), row2(gla_norm[l]))
        final = l == depth - 1

        qb, k_slab, v_slab, kb, vb, lru_in, gla_in = _inproj(
            xp, mod_p[l], row2(norm_mix[l]), w_mix, per_token=False, tok_per_seq=seq, tm=tm_p,
            kv_slab=(l, depth, k_slab, v_slab))
        y_sb = _sb_prompt(qb, kb, vb, sb_bias[l], batch=bp, seq=seq)
        y_lru, cv1, h1 = _lru(lru_in, zeros_conv, zeros_h, *lru_args, batch=bp, tok_per_seq=seq,
                              tc=tc_p, n_valid=tc_p, out_dtype=BF16)
        y_gla, s1 = _gla(gla_in, zeros_s, *gla_args, batch=bp, tok_per_seq=seq, c=c_gla,
                         n_valid=c_gla, out_dtype=BF16)
        xp = _merge(xp, mod_p[l], row2(norm_mix[l]), y_sb, y_lru, y_gla, w_gates, wbr, wout,
                    per_token=False, tok_per_seq=seq, tm=tm_p)
        xp = _ffn(xp, mod_p[l], row2(norm_ffn[l]), wgate, wup, wdown, row2(norm_final),
                  per_token=False, tok_per_seq=seq, tm=tm_p, final=final)
        outs["cvp"].append(cv1)
        outs["hp"].append(h1.reshape(bp, LRU_W))
        outs["sp"].append(s1.reshape(bp, GLA_H, GLA_DK, GLA_DV))

        qb, k, v, _, _, lru_in, gla_in = _inproj(
            xs, mod_s[l], row2(norm_mix[l]), w_mix, per_token=True, tok_per_seq=tsp, tm=tm_s)
        q4 = qb.reshape(bs, tsp, n_heads, SB_DH)[:, :n_new]
        eye = jnp.eye(n_heads, dtype=BF16)
        qbd = (q4[:, :, None, :, :] * eye[None, None, :, :, None]).reshape(bs, n_new * n_heads, SB_W)
        bias_rows = jnp.broadcast_to(jnp.tile(sb_bias[l], n_new)[:, None], (n_new * n_heads, 2 * PAGE))
        y_sb = _sb_sample(qbd, k, v, bias_rows, ck, cv, page_table, l, n_new=n_new, group=group)
        y_lru, cv2, h2 = _lru(lru_in, state_conv[l], state_lru[l].reshape(bs, 1, LRU_W), *lru_args,
                              batch=bs, tok_per_seq=tsp, tc=tsp, n_valid=n_new, out_dtype=F32)
        gla_pad = jnp.pad(gla_in.reshape(bs, tsp, GLA_IN_W),
                          ((0, 0), (0, GLA_SAMPLE_C - tsp), (0, 0))).reshape(bs * GLA_SAMPLE_C, GLA_IN_W)
        y_gla, s2 = _gla(gla_pad, state_gla[l].reshape(bs, GLA_KW, GLA_DV), *gla_args, batch=bs,
                         tok_per_seq=GLA_SAMPLE_C, c=GLA_SAMPLE_C, n_valid=n_new, out_dtype=F32)
        y_gla = y_gla.reshape(bs, GLA_SAMPLE_C, GLA_VW)[:, :tsp].reshape(bs * tsp, GLA_VW)
        xs = _merge(xs, mod_s[l], row2(norm_mix[l]), y_sb, y_lru, y_gla, w_gates, wbr, wout,
                    per_token=True, tok_per_seq=tsp, tm=tm_s)
        xs = _ffn(xs, mod_s[l], row2(norm_ffn[l]), wgate, wup, wdown, row2(norm_final),
                  per_token=True, tok_per_seq=tsp, tm=tm_s, final=final)
        outs["ks"].append(k.reshape(bs, tsp, n_heads, SB_DH)[:, :n_new])
        outs["vs"].append(v.reshape(bs, tsp, n_heads, SB_DH)[:, :n_new])
        outs["cvs"].append(cv2)
        outs["hs"].append(h2.reshape(bs, LRU_W))
        outs["ss"].append(s2.reshape(bs, GLA_H, GLA_DK, GLA_DV))

    y_prompt = xp.reshape(bp, seq, d)
    y_sample = xs.reshape(bs, tsp, d)[:, :n_new]
    st = lambda name: jnp.stack(outs[name])
    rows = lambda slab: jnp.transpose(slab.reshape(depth, bp, n_heads, SB_DH, seq), (0, 1, 4, 2, 3))
    return (y_prompt, y_sample, rows(k_slab), rows(v_slab), st("cvp"), st("hp"), st("sp"),
            st("ks"), st("vs"), st("cvs"), st("hs"), st("ss"))
```

```python
import functools

import numpy as np
import jax
import jax.numpy as jnp
from jax import lax
from jax.experimental import pallas as pl
from jax.experimental.pallas import tpu as pltpu

F32 = jnp.float32
BF16 = jnp.bfloat16

SB_DH = 64
SB_W = 512
LRU_W = 512
LRU_BLK = 64
CONV_TAPS = 4
LRU_SCALE = 8.0
GLA_H = 4
GLA_DK = 64
GLA_DV = 128
GLA_KW = GLA_H * GLA_DK
GLA_VW = GLA_H * GLA_DV
GLA_RANK = 16
GLA_TAU = 16.0
EPS = 1e-6
PAGE = 128
SB_TILE = 256

LANES = 128
SUBLANES = 8
VMEM_LIMIT = 56 * 1024 * 1024

SAMPLE_PAD_T = 8
GLA_SAMPLE_C = 32
GLA_IN_W = GLA_KW * 2 + GLA_VW * 2 + LANES
MIX_W = 3 * SB_W + 2 * LRU_W + GLA_IN_W


def _params(sem):
    return pltpu.CompilerParams(dimension_semantics=sem, vmem_limit_bytes=VMEM_LIMIT)


def _softplus(x):
    return jnp.maximum(x, 0.0) + jnp.log1p(jnp.exp(-jnp.abs(x)))


def _log_sigmoid(x):
    return jnp.minimum(x, 0.0) - jnp.log1p(jnp.exp(-jnp.abs(x)))


def _silu(x):
    return x * jax.nn.sigmoid(x)


def _gelu_tanh(x):
    return 0.5 * x * (1.0 + jnp.tanh(0.7978845608028654 * (x + 0.044715 * (x * x * x))))


def _split_bf16(x):
    hi = x.astype(BF16)
    lo = (x - hi.astype(F32)).astype(BF16)
    return hi, lo


def _mod_row(mod_ref, idx, per_token):
    if per_token:
        return mod_ref[idx]
    return mod_ref[idx:idx + 1, :]


def _norm_mod(x, g, shift, scale):
    xn = x * lax.rsqrt(jnp.mean(x * x, axis=-1, keepdims=True) + EPS) * g
    return xn * (1.0 + scale) + shift


def _mod_spec(per_token, tm, d, tok_per_seq):
    if per_token:
        return pl.BlockSpec((6, tm, d), lambda i, *_: (0, i, 0))
    blocks_per_seq = tok_per_seq // tm
    return pl.BlockSpec((None, 6, d), lambda i, *_: (i // blocks_per_seq, 0, 0))


def _ada_kernel(c_ref, w_ref, b_ref, o_ref):
    c = c_ref[...]
    s = _silu(c).astype(BF16)
    o_ref[...] = jnp.dot(s, w_ref[...].astype(BF16), preferred_element_type=F32) + b_ref[...]


def _ada(c_all, w_ada, b_ada):
    depth, d, e = w_ada.shape
    n = c_all.shape[0]
    tn = 1536 if e % 1536 == 0 else e
    return pl.pallas_call(
        _ada_kernel,
        grid=(depth, e // tn),
        in_specs=[
            pl.BlockSpec((n, d), lambda l, j: (0, 0)),
            pl.BlockSpec((None, d, tn), lambda l, j: (l, 0, j)),
            pl.BlockSpec((None, 1, tn), lambda l, j: (l, 0, j)),
        ],
        out_specs=pl.BlockSpec((None, n, tn), lambda l, j: (l, 0, j)),
        out_shape=jax.ShapeDtypeStruct((depth, n, e), F32),
        compiler_params=_params(("parallel", "parallel")),
    )(c_all, w_ada, b_ada.reshape(depth, 1, e))


def _inproj_kernel(x_ref, mod_ref, g_ref, w_ref, *rest, per_token, kv_slab):
    q_ref, k_ref, v_ref, kb_ref, vb_ref, lru_ref, gla_ref = rest[-7:]
    h = _norm_mod(x_ref[...], g_ref[...], _mod_row(mod_ref, 0, per_token),
                  _mod_row(mod_ref, 1, per_token)).astype(BF16)

    def seg(a, b):
        return jnp.dot(h, w_ref[:, a:b], preferred_element_type=F32)

    q = seg(0, SB_W) * (SB_DH ** -0.5)
    k = seg(SB_W, 2 * SB_W)
    v = seg(2 * SB_W, 3 * SB_W)
    if kv_slab:
        perm, perm_t = rest[0][...], rest[1][...]
        q_ref[...] = q.T.astype(BF16)
        k_ref[...] = k.T
        v_t = v.T
        v_ref[...] = v_t
        kb = k.astype(BF16)
        vb = v.astype(BF16)
        for t in range(k.shape[0] // SB_TILE):
            sl = slice(t * SB_TILE, (t + 1) * SB_TILE)
            kb_ref[sl, :] = jnp.dot(perm, kb[sl, :], preferred_element_type=F32).astype(BF16)
            vb_ref[t] = lax.dot_general(vb[sl, :], perm_t, (((0,), (0,)), ((), ())),
                                        preferred_element_type=F32).astype(BF16)
    else:
        q_ref[...] = q.astype(BF16)
        k_ref[...] = k
        kb_ref[...] = k.astype(BF16)
        v_ref[...] = v
        vb_ref[...] = v.astype(BF16)
    o = 3 * SB_W
    lru_ref[...] = seg(o, o + 2 * LRU_W)
    o += 2 * LRU_W
    gla_ref[...] = seg(o, o + GLA_IN_W)


def _key_permutation():
    groups = SB_TILE // SUBLANES
    p = np.zeros((SB_TILE, SB_TILE), np.float32)
    for r in range(groups):
        for sub in range(SUBLANES):
            p[SUBLANES * r + sub, groups * sub + r] = 1.0
    return jnp.asarray(p, dtype=BF16)


def _inproj(x, mod, g, w_mix, *, per_token, tok_per_seq, tm, kv_slab=None):
    n, d = x.shape
    row = lambda w: pl.BlockSpec((tm, w), lambda i: (i, 0))
    outs = [(SB_W, BF16), (SB_W, F32), (SB_W, F32), (SB_W, BF16), (SB_W, BF16),
            (2 * LRU_W, F32), (GLA_IN_W, F32)]
    out_specs = [row(w) for w, _ in outs]
    out_shape = [jax.ShapeDtypeStruct((n, w), dt) for w, dt in outs]
    extra_in, extra_specs, aliases = [], [], {}
    if kv_slab is not None:
        layer, depth, k_slab, v_slab = kv_slab
        bps = tok_per_seq // tm
        batch = n // tok_per_seq
        slab_spec = pl.BlockSpec((None, None, SB_W, tm), lambda i: (layer, i // bps, 0, i % bps))
        slab_shape = jax.ShapeDtypeStruct((depth, batch, SB_W, tok_per_seq), F32)
        out_specs[1] = out_specs[2] = slab_spec
        out_shape[1] = out_shape[2] = slab_shape
        out_specs[0] = pl.BlockSpec((None, SB_W, tm), lambda i: (i // bps, 0, i % bps))
        out_shape[0] = jax.ShapeDtypeStruct((batch, SB_W, tok_per_seq), BF16)
        tpb = tm // SB_TILE
        out_specs[4] = pl.BlockSpec((None, tpb, SB_W, SB_TILE), lambda i: (i // bps, i % bps, 0, 0))
        out_shape[4] = jax.ShapeDtypeStruct((batch, tok_per_seq // SB_TILE, SB_W, SB_TILE), BF16)
        perm = _key_permutation()
        extra_in = [perm, perm.T]
        extra_specs = [pl.BlockSpec((SB_TILE, SB_TILE), lambda i: (0, 0))] * 2
        if k_slab is not None:
            extra_in += [k_slab, v_slab]
            extra_specs += [pl.BlockSpec(memory_space=pl.ANY)] * 2
            aliases = {6: 1, 7: 2}
    return pl.pallas_call(
        functools.partial(_inproj_kernel, per_token=per_token, kv_slab=kv_slab is not None),
        grid=(n // tm,),
        in_specs=[
            row(d),
            _mod_spec(per_token, tm, d, tok_per_seq),
            pl.BlockSpec((1, d), lambda i: (0, 0)),
            pl.BlockSpec((d, MIX_W), lambda i: (0, 0)),
        ] + extra_specs,
        out_specs=out_specs,
        out_shape=out_shape,
        input_output_aliases=aliases,
        compiler_params=_params(("parallel",)),
    )(x, mod, g, w_mix, *extra_in)


def _sb_logs(z):
    t = jnp.log(1.0 + jnp.exp(-jnp.abs(z)))
    m = jnp.minimum(z, 0.0)
    return m - t, (m - z) - t


def _sb_logs_neg(z):
    neg_log_keep = jnp.maximum(z, 0.0) + jnp.log(1.0 + jnp.exp(-jnp.abs(z)))
    return z - neg_log_keep, neg_log_keep


def _suffix_lhs(log_keep, m2):
    if m2.shape[0] == log_keep.shape[1]:
        return log_keep.astype(BF16)
    return jnp.concatenate(_split_bf16(log_keep), axis=1)


def _sb_tile(qh, kblk, vblk, bias, m2, carry, acc, valid):
    z = lax.dot_general(qh, kblk, (((1,), (1,)), ((), ())), preferred_element_type=F32) + bias
    log_beta, log_keep = _sb_logs(z)
    if valid is not None:
        log_keep = jnp.where(valid, log_keep, 0.0)
    sfx = jnp.dot(_suffix_lhs(log_keep, m2), m2, preferred_element_type=F32)
    a = jnp.exp(log_beta + sfx + carry)
    if valid is not None:
        a = jnp.where(valid, a, 0.0)
    carry = carry + jnp.sum(log_keep, axis=1, keepdims=True)
    acc = acc + jnp.dot(a.astype(BF16), vblk, preferred_element_type=F32)
    return carry, acc


def _sb_tiles_keys_major(q2t, k_rows, vt_cols, bias_row, top_valid):
    n_tiles = k_rows.shape[0] // SB_TILE
    groups = SB_TILE // SUBLANES
    width = q2t.shape[1]
    zt = jnp.dot(k_rows, q2t, preferred_element_type=F32) + bias_row
    log_beta, log_keep = _sb_logs_neg(zt)
    sub = lax.broadcasted_iota(jnp.int32, (SUBLANES, width), 0)
    carry = jnp.zeros((SUBLANES, width), F32)
    a_tiles = [None] * n_tiles
    for t in reversed(range(n_tiles)):
        top = t == n_tiles - 1
        rows = [slice(t * SB_TILE + SUBLANES * r, t * SB_TILE + SUBLANES * (r + 1)) for r in range(groups)]
        masks = [top_valid[SUBLANES * r:SUBLANES * (r + 1)] for r in range(groups)] if top else None
        below = [None] * groups
        run = jnp.zeros((SUBLANES, width), F32)
        for r in reversed(range(groups)):
            below[r] = run
            lk = log_keep[rows[r]]
            run = run + (jnp.where(masks[r], lk, 0.0) if top else lk)
        inc = run
        for d in (1, 2, 4):
            inc = inc + jnp.where(sub < SUBLANES - d, pltpu.roll(inc, SUBLANES - d, axis=0), 0.0)
        later = (inc - run) + carry
        carry = carry + jnp.broadcast_to(inc[0:1], (SUBLANES, width))
        a_rows = [jnp.exp(log_beta[rows[r]] - below[r] - later) for r in range(groups)]
        if top:
            a_rows = [jnp.where(masks[r], a_rows[r], 0.0) for r in range(groups)]
        a_tiles[t] = jnp.concatenate(a_rows, axis=0)
    a = a_tiles[0] if n_tiles == 1 else jnp.concatenate(a_tiles, axis=0)
    return jnp.dot(vt_cols, a.astype(BF16), preferred_element_type=F32)


def _sb_prompt_kernel(bias_ref, qt_ref, k_ref, vt_ref, o_ref, *, tq, nq, qb):
    hp = pl.program_id(1)
    gi = pl.program_id(2)
    qt = qt_ref[...].astype(F32)
    feat = lax.broadcasted_iota(jnp.int32, (LANES, tq), 0)
    lane1 = lax.broadcasted_iota(jnp.int32, (1, 2 * tq), 1)
    bias_row = jnp.where(lane1 < tq, bias_ref[2 * hp], bias_ref[2 * hp + 1])
    row = lax.broadcasted_iota(jnp.int32, (tq, 2 * tq), 0)
    lane = lax.broadcasted_iota(jnp.int32, (tq, 2 * tq), 1)
    key = (row & (SUBLANES - 1)) * (tq // SUBLANES) + (row >> 3)
    diag_valid = key < (lane & (tq - 1))

    def blocks(g):
        outs = []
        for s in range(qb):
            n = g * qb + s + 1
            q = qt[:, s * tq:(s + 1) * tq]
            q2t = jnp.concatenate([jnp.where(feat < SB_DH, q, 0.0), jnp.where(feat >= SB_DH, q, 0.0)],
                                  axis=1).astype(BF16)
            vt = vt_ref[0] if n == 1 else jnp.concatenate([vt_ref[t] for t in range(n)], axis=1)
            acc = _sb_tiles_keys_major(q2t, k_ref[0:n * tq, :], vt, bias_row, diag_valid)
            outs.append(jnp.concatenate([acc[:SB_DH, :tq], acc[SB_DH:, tq:]], axis=0).T)
        return outs[0] if qb == 1 else jnp.concatenate(outs, axis=0)

    out = lax.switch(gi, [functools.partial(blocks, g) for g in range(nq // qb)])
    o_ref[...] = out.astype(o_ref.dtype)


def _suffix_matrix(t, parts):
    m = (np.arange(t)[:, None] > np.arange(t)[None, :]).astype(np.float32)
    return jnp.asarray(np.concatenate([m] * parts, axis=0), dtype=BF16)


def _sb_prompt(qt, kperm, vt_tiles, bias, *, batch, seq):
    tq = SB_TILE
    nq = seq // tq
    assert nq <= 16, "one unrolled body per query block: sequence too long for this kernel"
    qb = 2 if nq % 2 == 0 else 1
    steps = nq // qb
    return pl.pallas_call(
        functools.partial(_sb_prompt_kernel, tq=tq, nq=nq, qb=qb),
        grid=(batch, SB_W // LANES, steps),
        in_specs=[
            pl.BlockSpec(memory_space=pltpu.SMEM),
            pl.BlockSpec((None, LANES, qb * tq), lambda b, h, i: (b, h, i)),
            pl.BlockSpec((seq, LANES), lambda b, h, i: (b, h)),
            pl.BlockSpec((None, nq, LANES, tq), lambda b, h, i: (b, 0, h, 0)),
        ],
        out_specs=pl.BlockSpec((qb * tq, LANES), lambda b, h, i: (b * steps + i, h)),
        out_shape=jax.ShapeDtypeStruct((batch * seq, SB_W), BF16),
        compiler_params=_params(("parallel", "parallel", "arbitrary")),
    )(bias, qt, kperm, vt_tiles)


def _sb_sample_kernel(pt_ref, qbd_ref, kn_ref, vn_ref, bias_ref, m2_ref, *refs, group, n_new, n_heads):
    del pt_ref
    kp = refs[:group]
    vp = refs[group:2 * group]
    o_ref = refs[2 * group]
    carry_ref, acc_ref = refs[2 * group + 1:]
    p = pl.program_id(1)
    qbd = qbd_ref[...]
    bias = bias_ref[...]
    m2 = m2_ref[...]
    rows = n_new * n_heads
    width = 2 * PAGE
    n_pair = group // 2

    @pl.when(p == 0)
    def _():
        pad = jnp.zeros((width - SAMPLE_PAD_T, SB_W), F32)
        kn = jnp.concatenate([kn_ref[...], pad], axis=0).astype(BF16)
        vn = jnp.concatenate([vn_ref[...], pad], axis=0).astype(BF16)
        row = lax.broadcasted_iota(jnp.int32, (rows, width), 0)
        col = lax.broadcasted_iota(jnp.int32, (rows, width), 1)
        valid = col < row // n_heads
        carry, acc = _sb_tile(qbd, kn, vn, bias, m2, jnp.zeros((rows, 1), F32),
                              jnp.zeros((rows, SB_W), F32), valid)
        carry_ref[...] = carry
        acc_ref[...] = acc

    def pair(page_refs, j):
        low = page_refs[group - 1 - 2 * j][...]
        high = page_refs[group - 2 - 2 * j][...]
        return jnp.concatenate([low, high], axis=1).astype(BF16)

    logs = [_sb_logs(jnp.dot(qbd, pair(kp, j), preferred_element_type=F32) + bias) for j in range(n_pair)]
    split = [jnp.concatenate(_split_bf16(lk), axis=1) for _, lk in logs]
    sfx = jnp.dot(jnp.concatenate(split, axis=0), m2, preferred_element_type=F32)
    carry = carry_ref[...]
    carries = [None] * n_pair
    for j in reversed(range(n_pair)):
        carries[j] = carry
        carry = carry + jnp.sum(logs[j][1], axis=1, keepdims=True)
    acc = acc_ref[...]
    for j in range(n_pair):
        a = jnp.exp(logs[j][0] + sfx[j * rows:(j + 1) * rows] + carries[j]).astype(BF16)
        acc = acc + lax.dot_general(a, pair(vp, j), (((1,), (1,)), ((), ())), preferred_element_type=F32)
    carry_ref[...] = carry
    acc_ref[...] = acc

    @pl.when(p == pl.num_programs(1) - 1)
    def _():
        row = lax.broadcasted_iota(jnp.int32, (rows, SB_W), 0)
        col = lax.broadcasted_iota(jnp.int32, (rows, SB_W), 1)
        own = jnp.where(col // SB_DH == row % n_heads, acc, 0.0)
        o_ref[...] = jnp.zeros(o_ref.shape, o_ref.dtype)
        for qq in range(n_new):
            o_ref[qq:qq + 1, :] = jnp.sum(own[qq * n_heads:(qq + 1) * n_heads, :], axis=0, keepdims=True)


def _sb_sample(qbd, k_new, v_new, bias_rows, cache_k, cache_v, page_table, layer, *, n_new, group):
    bs, n_pages = page_table.shape
    n_heads = SB_W // SB_DH
    rows = n_new * n_heads
    steps = n_pages // group

    def page_spec(i):
        def index(b, p, pt):
            return (layer, pt[b, n_pages - 1 - (p * group + i)], 0, 0)
        return pl.BlockSpec((None, None, SB_W, PAGE), index)

    grid_spec = pltpu.PrefetchScalarGridSpec(
        num_scalar_prefetch=1,
        grid=(bs, steps),
        in_specs=[
            pl.BlockSpec((None, rows, SB_W), lambda b, p, pt: (b, 0, 0)),
            pl.BlockSpec((SAMPLE_PAD_T, SB_W), lambda b, p, pt: (b, 0)),
            pl.BlockSpec((SAMPLE_PAD_T, SB_W), lambda b, p, pt: (b, 0)),
            pl.BlockSpec((rows, 2 * PAGE), lambda b, p, pt: (0, 0)),
            pl.BlockSpec((4 * PAGE, 2 * PAGE), lambda b, p, pt: (0, 0)),
        ] + [page_spec(i) for i in range(group)] * 2,
        out_specs=pl.BlockSpec((SAMPLE_PAD_T, SB_W), lambda b, p, pt: (b, 0)),
        scratch_shapes=[pltpu.VMEM((rows, 1), F32), pltpu.VMEM((rows, SB_W), F32)],
    )
    return pl.pallas_call(
        functools.partial(_sb_sample_kernel, group=group, n_new=n_new, n_heads=n_heads),
        grid_spec=grid_spec,
        out_shape=jax.ShapeDtypeStruct((bs * SAMPLE_PAD_T, SB_W), F32),
        compiler_params=_params(("parallel", "arbitrary")),
    )(page_table, qbd, k_new, v_new, bias_rows, _suffix_matrix(2 * PAGE, 2),
      *([cache_k] * group), *([cache_v] * group))


def _lru_kernel(x_ref, conv0_ref, h0_ref, cw_ref, cb_ref, wg_ref, ba_ref, bx_ref, lam_ref,
                y_ref, conv_ref, hl_ref, xpad, a_s, b_s, hs_s, hcar, *, tc, n_valid):
    ci = pl.program_id(1)
    head = SUBLANES
    tail = CONV_TAPS - 1

    @pl.when(ci == 0)
    def _():
        xpad[head - tail:head, :] = conv0_ref[...]
        hcar[...] = h0_ref[...]

    xpad[head:head + tc, :] = x_ref[:, 0:LRU_W]
    xc = cb_ref[...] + xpad[head - tail:head - tail + tc, :] * cw_ref[0:1, :]
    for w in range(1, CONV_TAPS):
        xc = xc + xpad[head - tail + w:head - tail + w + tc, :] * cw_ref[w:w + 1, :]
    xcb = xc.astype(BF16)
    neg_sp = -LRU_SCALE * _softplus(-lam_ref[...])
    for j in range(LRU_W // LANES):
        sl = slice(j * LANES, (j + 1) * LANES)
        g = jnp.dot(xcb[:, sl], wg_ref[j], preferred_element_type=F32)
        r = jax.nn.sigmoid(g[:, :LANES] + ba_ref[:, sl])
        gate_in = jax.nn.sigmoid(g[:, LANES:] + bx_ref[:, sl])
        log_a = neg_sp[:, sl] * r
        a = jnp.exp(log_a)
        mult = jnp.sqrt(-jnp.tanh(log_a) * (a * a + 1.0))
        a_s[:, sl] = a
        b_s[:, sl] = mult * (gate_in * xc[:, sl])

    rowi = lax.broadcasted_iota(jnp.int32, (SUBLANES, LRU_W), 0)

    def tile_scan(g, h):
        r0 = pl.multiple_of(g * SUBLANES, SUBLANES)
        a = a_s[pl.ds(r0, SUBLANES), :]
        b = b_s[pl.ds(r0, SUBLANES), :]
        for d in (1, 2, 4):
            keep = rowi >= d
            b = jnp.where(keep, a * pltpu.roll(b, d, axis=0) + b, b)
            a = jnp.where(keep, a * pltpu.roll(a, d, axis=0), a)
        hs = a * h + b
        hs_s[pl.ds(r0, SUBLANES), :] = hs
        return hs[SUBLANES - 1:SUBLANES, :]

    hcar[...] = lax.fori_loop(0, tc // SUBLANES, tile_scan, hcar[...])
    y_ref[...] = (hs_s[...] * _gelu_tanh(x_ref[:, LRU_W:2 * LRU_W])).astype(y_ref.dtype)

    new_tail = xpad[head + n_valid - tail:head + n_valid, :]
    conv_ref[...] = new_tail
    hl_ref[...] = hs_s[n_valid - 1:n_valid, :]
    xpad[head - tail:head, :] = new_tail


def _lru(lru_in, conv0, h0, cw, cb, wg, ba, bx, lam, *, batch, tok_per_seq, tc, n_valid, out_dtype):
    n = lru_in.shape[0]
    nc = tok_per_seq // tc
    vec = pl.BlockSpec((1, LRU_W), lambda b, c: (0, 0))
    tail = CONV_TAPS - 1
    return pl.pallas_call(
        functools.partial(_lru_kernel, tc=tc, n_valid=n_valid),
        grid=(batch, nc),
        in_specs=[
            pl.BlockSpec((tc, 2 * LRU_W), lambda b, c: (b * nc + c, 0)),
            pl.BlockSpec((None, tail, LRU_W), lambda b, c: (b, 0, 0)),
            pl.BlockSpec((None, 1, LRU_W), lambda b, c: (b, 0, 0)),
            pl.BlockSpec((CONV_TAPS, LRU_W), lambda b, c: (0, 0)),
            vec,
            pl.BlockSpec((LRU_W // LANES, LANES, 2 * LANES), lambda b, c: (0, 0, 0)),
            vec, vec, vec,
        ],
        out_specs=[
            pl.BlockSpec((tc, LRU_W), lambda b, c: (b * nc + c, 0)),
            pl.BlockSpec((None, tail, LRU_W), lambda b, c: (b, 0, 0)),
            pl.BlockSpec((None, 1, LRU_W), lambda b, c: (b, 0, 0)),
        ],
        out_shape=[
            jax.ShapeDtypeStruct((n, LRU_W), out_dtype),
            jax.ShapeDtypeStruct((batch, tail, LRU_W), F32),
            jax.ShapeDtypeStruct((batch, 1, LRU_W), F32),
        ],
        scratch_shapes=[
            pltpu.VMEM((SUBLANES + tc, LRU_W), F32),
            pltpu.VMEM((tc, LRU_W), F32),
            pltpu.VMEM((tc, LRU_W), F32),
            pltpu.VMEM((tc, LRU_W), F32),
            pltpu.VMEM((1, LRU_W), F32),
        ],
        compiler_params=_params(("parallel", "arbitrary")),
    )(lru_in, conv0, h0, cw, cb, wg, ba, bx, lam)


def _gla_kernel(x_ref, s0_ref, wal_ref, bal_ref, gn_ref, tri_ref, y_ref, sout_ref, st_ref, *, c, n_valid, nseq):
    ci = pl.program_id(1)
    seqs = range(nseq)
    nt = (((1,), (1,)), ((), ()))

    @pl.when(ci == 0)
    def _():
        for s in seqs:
            st_ref[s] = s0_ref[s].T

    xs = [x_ref[s] for s in seqs]
    o_v = 2 * GLA_KW
    o_go = o_v + GLA_VW
    o_lr = o_go + GLA_VW
    glr = jnp.concatenate([x[:, o_lr:] for x in xs], axis=0).astype(BF16)
    xl = jnp.dot(glr, wal_ref[...], preferred_element_type=F32) + bal_ref[...]
    la = _log_sigmoid(xl) * (1.0 / GLA_TAU)
    ks = [x[:, GLA_KW:o_v] for x in xs]
    las = [la[s * c:(s + 1) * c] for s in seqs]
    if n_valid < c:
        rowi = lax.broadcasted_iota(jnp.int32, (c, GLA_KW), 0)
        las = [jnp.where(rowi < n_valid, l, 0.0) for l in las]
        ks = [jnp.where(rowi < n_valid, k, 0.0) for k in ks]
    parts = [_split_bf16(l) for l in las]
    tri = tri_ref[...]
    g_all = (jnp.dot(tri, jnp.concatenate([p[0] for p in parts], axis=1), preferred_element_type=F32)
             + jnp.dot(tri, jnp.concatenate([p[1] for p in parts], axis=1), preferred_element_type=F32))

    lane_k = lax.broadcasted_iota(jnp.int32, (1, GLA_KW), 1) // GLA_DK
    row = lax.broadcasted_iota(jnp.int32, (c, GLA_H * c), 0)
    col = lax.broadcasted_iota(jnp.int32, (c, GLA_H * c), 1)
    causal = col % c <= row
    col_h = col // c

    def by_head(a):
        return jnp.concatenate([jnp.where(lane_k == h, a, 0.0) for h in range(GLA_H)], axis=0)

    q_ins, q_rels, kbd_rels, kbd_decs, decays = [], [], [], [], []
    for s in seqs:
        g = g_all[:, s * GLA_KW:(s + 1) * GLA_KW]
        g_last = g[c - 1:c, :]
        g_mid = g[c // 2:c // 2 + 1, :]
        q = xs[s][:, 0:GLA_KW] * (GLA_DK ** -0.5)
        q_ins.append(q * jnp.exp(g))
        q_rels.append((q * jnp.exp(g - g_mid)).astype(BF16))
        kbd_rels.append(by_head(ks[s] * jnp.exp(g_mid - g)).astype(BF16))
        kbd_decs.append(by_head(ks[s] * jnp.exp(g_last - g)).astype(BF16))
        decays.append(jnp.exp(g_last))

    atts = [lax.dot_general(q_rels[s], kbd_rels[s], nt, preferred_element_type=F32) for s in seqs]
    v_ts, w_ts, lhss = [], [], []
    for s in seqs:
        v = xs[s][:, o_v:o_go]
        v_t = jnp.concatenate([v[:, h * GLA_DV:(h + 1) * GLA_DV] for h in range(GLA_H)], axis=0).T
        v_ts.append(v_t)
        w_ts.append(jnp.concatenate([st_ref[s], v_t], axis=1).astype(BF16))
        att = jnp.where(causal, atts[s], 0.0)
        lhss.append(jnp.concatenate(
            [jnp.concatenate([jnp.where(lane_k == h, q_ins[s], 0.0), jnp.where(col_h == h, att, 0.0)], axis=1)
             for h in range(GLA_H)], axis=0).astype(BF16))
    ress = [lax.dot_general(lhss[s], w_ts[s], nt, preferred_element_type=F32) for s in seqs]
    upds = [jnp.dot(v_ts[s].astype(BF16), kbd_decs[s], preferred_element_type=F32) for s in seqs]

    for s in seqs:
        st_new = st_ref[s] * decays[s] + upds[s]
        st_ref[s] = st_new
        outs = []
        for h in range(GLA_H):
            oh = ress[s][h * c:(h + 1) * c, :]
            oh = oh * lax.rsqrt(jnp.mean(oh * oh, axis=-1, keepdims=True) + EPS)
            outs.append(oh * gn_ref[:, h * GLA_DV:(h + 1) * GLA_DV])
        go = xs[s][:, o_go:o_lr]
        y_ref[s] = (jnp.concatenate(outs, axis=1) * _silu(go)).astype(y_ref.dtype)

    @pl.when(ci == pl.num_programs(1) - 1)
    def _():
        for s in seqs:
            sout_ref[s] = st_ref[s].T


def _gla(gla_in, s0, wal, bal, gn, *, batch, tok_per_seq, c, n_valid, out_dtype):
    nc = tok_per_seq // c
    nseq = _pick_tile(batch, 4)
    tri = jnp.asarray((np.arange(c)[:, None] >= np.arange(c)[None, :]).astype(np.float32), dtype=BF16)
    state = pl.BlockSpec((nseq, GLA_KW, GLA_DV), lambda b, i: (b, 0, 0))
    y, s_out = pl.pallas_call(
        functools.partial(_gla_kernel, c=c, n_valid=n_valid, nseq=nseq),
        grid=(batch // nseq, nc),
        in_specs=[
            pl.BlockSpec((nseq, c, GLA_IN_W), lambda b, i: (b, i, 0)),
            state,
            pl.BlockSpec((LANES, GLA_KW), lambda b, i: (0, 0)),
            pl.BlockSpec((1, GLA_KW), lambda b, i: (0, 0)),
            pl.BlockSpec((1, GLA_VW), lambda b, i: (0, 0)),
            pl.BlockSpec((c, c), lambda b, i: (0, 0)),
        ],
        out_specs=[pl.BlockSpec((nseq, c, GLA_VW), lambda b, i: (b, i, 0)), state],
        out_shape=[
            jax.ShapeDtypeStruct((batch, tok_per_seq, GLA_VW), out_dtype),
            jax.ShapeDtypeStruct((batch, GLA_KW, GLA_DV), F32),
        ],
        scratch_shapes=[pltpu.VMEM((nseq, GLA_DV, GLA_KW), F32)],
        compiler_params=_params(("parallel", "arbitrary")),
    )(gla_in.reshape(batch, tok_per_seq, GLA_IN_W), s0, wal, bal, gn, tri)
    return y.reshape(batch * tok_per_seq, GLA_VW), s_out


def _merge_kernel(x_ref, mod_ref, g_ref, ysb_ref, ylru_ref, ygla_ref, wg_ref, wbr_ref, wout_ref,
                  o_ref, *, per_token):
    x = x_ref[...]
    d = x.shape[1]
    h = _norm_mod(x, g_ref[...], _mod_row(mod_ref, 0, per_token),
                  _mod_row(mod_ref, 1, per_token)).astype(BF16)
    merged = None
    for b, y_ref in enumerate((ysb_ref, ylru_ref, ygla_ref)):
        gate = jax.nn.sigmoid(jnp.dot(h, wg_ref[:, b * d:(b + 1) * d], preferred_element_type=F32))
        br = jnp.dot(y_ref[...].astype(BF16), wbr_ref[b], preferred_element_type=F32)
        merged = gate * br if merged is None else merged + gate * br
    out = jnp.dot(merged.astype(BF16), wout_ref[...], preferred_element_type=F32)
    o_ref[...] = x + _mod_row(mod_ref, 2, per_token) * out


def _merge(x, mod, g, ysb, ylru, ygla, wg, wbr, wout, *, per_token, tok_per_seq, tm):
    n, d = x.shape
    row = lambda w: pl.BlockSpec((tm, w), lambda i: (i, 0))
    return pl.pallas_call(
        functools.partial(_merge_kernel, per_token=per_token),
        grid=(n // tm,),
        in_specs=[
            row(d),
            _mod_spec(per_token, tm, d, tok_per_seq),
            pl.BlockSpec((1, d), lambda i: (0, 0)),
            row(SB_W), row(LRU_W), row(GLA_VW),
            pl.BlockSpec((d, 3 * d), lambda i: (0, 0)),
            pl.BlockSpec((3, SB_W, d), lambda i: (0, 0, 0)),
            pl.BlockSpec((d, d), lambda i: (0, 0)),
        ],
        out_specs=row(d),
        out_shape=jax.ShapeDtypeStruct((n, d), F32),
        compiler_params=_params(("parallel",)),
    )(x, mod, g, ysb, ylru, ygla, wg, wbr, wout)


def _ffn_kernel(x_ref, mod_ref, g_ref, wgate_ref, wup_ref, wdown_ref, gfin_ref, o_ref, h_s, acc_s,
                *, per_token, final):
    f = pl.program_id(1)

    @pl.when(f == 0)
    def _():
        h_s[...] = _norm_mod(x_ref[...], g_ref[...], _mod_row(mod_ref, 3, per_token),
                             _mod_row(mod_ref, 4, per_token)).astype(BF16)
        acc_s[...] = jnp.zeros(acc_s.shape, F32)

    h = h_s[...]
    ff = _silu(jnp.dot(h, wgate_ref[...], preferred_element_type=F32)) * jnp.dot(
        h, wup_ref[...], preferred_element_type=F32)
    acc_s[...] += jnp.dot(ff.astype(BF16), wdown_ref[...], preferred_element_type=F32)

    @pl.when(f == pl.num_programs(1) - 1)
    def _():
        x2 = x_ref[...] + _mod_row(mod_ref, 5, per_token) * acc_s[...]
        if final:
            x2 = x2 * lax.rsqrt(jnp.mean(x2 * x2, axis=-1, keepdims=True) + EPS) * gfin_ref[...]
        o_ref[...] = x2


def _ffn(x, mod, g, wgate, wup, wdown, gfin, *, per_token, tok_per_seq, tm, final):
    n, d = x.shape
    dff = wgate.shape[1]
    tf = dff
    once = dict(pipeline_mode=pl.Buffered(1))
    mod_spec = _mod_spec(per_token, tm, d, tok_per_seq)
    return pl.pallas_call(
        functools.partial(_ffn_kernel, per_token=per_token, final=final),
        grid=(n // tm, dff // tf),
        in_specs=[
            pl.BlockSpec((tm, d), lambda i, f: (i, 0)),
            mod_spec,
            pl.BlockSpec((1, d), lambda i, f: (0, 0)),
            pl.BlockSpec((d, tf), lambda i, f: (0, f), **once),
            pl.BlockSpec((d, tf), lambda i, f: (0, f), **once),
            pl.BlockSpec((tf, d), lambda i, f: (f, 0), **once),
            pl.BlockSpec((1, d), lambda i, f: (0, 0)),
        ],
        out_specs=pl.BlockSpec((tm, d), lambda i, f: (i, 0)),
        out_shape=jax.ShapeDtypeStruct((n, d), F32),
        scratch_shapes=[pltpu.VMEM((tm, d), BF16), pltpu.VMEM((tm, d), F32)],
        compiler_params=_params(("parallel", "arbitrary")),
    )(x, mod, g, wgate, wup, wdown, gfin)


def _split_w_in(w_in_l):
    d = w_in_l.shape[0]
    sizes = (SB_W, SB_W, SB_W, LRU_W, LRU_W, GLA_KW, GLA_KW, GLA_VW, GLA_VW, GLA_RANK, d, d, d)
    offs = np.concatenate([[0], np.cumsum(sizes)])
    seg = lambda i: w_in_l[:, offs[i]:offs[i + 1]]
    glr = jnp.pad(seg(9), ((0, 0), (0, LANES - GLA_RANK)))
    w_mix = jnp.concatenate([seg(i) for i in range(9)] + [glr], axis=1).astype(BF16)
    w_gates = jnp.concatenate([seg(10), seg(11), seg(12)], axis=1).astype(BF16)
    return w_mix, w_gates


def _lru_gate_tiles(wa, wx):
    nb = wa.shape[0]
    z = jnp.zeros((LRU_BLK, LRU_BLK), F32)

    def pair(w, j):
        return jnp.concatenate([jnp.concatenate([w[2 * j], z], axis=1),
                                jnp.concatenate([z, w[2 * j + 1]], axis=1)], axis=0)

    tiles = [jnp.concatenate([pair(wa, j), pair(wx, j)], axis=1) for j in range(nb // 2)]
    return jnp.stack(tiles).astype(BF16)


def _pick_tile(n, pref):
    t = pref
    while n % t:
        t //= 2
    return t


def kernel(x_prompt, x_sample, c_prompt, c_sample, cache_k, cache_v, page_table, state_conv, state_lru, state_gla, w_ada, b_ada, norm_mix, w_in, sb_bias, conv_w, conv_b, lru_wa, lru_ba, lru_wx, lru_bx, lru_lambda, gla_w_alpha, gla_b_alpha, gla_norm, w_br_sb, w_br_lru, w_br_gla, w_out, norm_ffn, w_gate, w_up, w_down, norm_final):
    bp, seq, d = x_prompt.shape
    bs, n_new, _ = x_sample.shape
    depth = w_ada.shape[0]
    n_heads = SB_W // SB_DH
    n_pool = cache_k.shape[1]
    tsp = SAMPLE_PAD_T

    xp = x_prompt.reshape(bp * seq, d)
    xs = jnp.pad(x_sample, ((0, 0), (0, tsp - n_new), (0, 0))).reshape(bs * tsp, d)
    ck = jnp.transpose(cache_k, (0, 1, 3, 4, 2)).reshape(depth, n_pool, SB_W, PAGE)
    cv = jnp.transpose(cache_v, (0, 1, 3, 4, 2)).reshape(depth, n_pool, SB_W, PAGE)

    mod = _ada(jnp.concatenate([c_prompt, c_sample], axis=0), w_ada, b_ada)
    mod = mod.reshape(depth, bp + bs, 6, d)
    mod_p = mod[:, :bp]
    mod_s = jnp.transpose(jnp.repeat(mod[:, bp:], tsp, axis=1), (0, 2, 1, 3))

    tm_p = _pick_tile(seq, 512)
    tm_s = bs * tsp
    tc_p = _pick_tile(seq, 512)
    c_gla = _pick_tile(seq, 64)
    group = _pick_tile(page_table.shape[1], 16)

    zeros_conv = jnp.zeros((bp, CONV_TAPS - 1, LRU_W), F32)
    zeros_h = jnp.zeros((bp, 1, LRU_W), F32)
    zeros_s = jnp.zeros((bp, GLA_KW, GLA_DV), F32)
    row2 = lambda a: a.reshape(1, -1)

    outs = {k: [] for k in ("cvp", "hp", "sp", "ks", "vs", "cvs", "hs", "ss")}
    k_slab = v_slab = None
    for l in range(depth):
        w_mix, w_gates = _split_w_in(w_in[l])
        wg_lru = _lru_gate_tiles(lru_wa[l], lru_wx[l])
        wal = jnp.pad(gla_w_alpha[l], ((0, LANES - GLA_RANK), (0, 0))).astype(BF16)
        wbr = jnp.stack([w_br_sb[l], w_br_lru[l], w_br_gla[l]]).astype(BF16)
        wout = w_out[l].astype(BF16)
        wgate, wup, wdown = w_gate[l].astype(BF16), w_up[l].astype(BF16), w_down[l].astype(BF16)
        lru_args = (conv_w[l], row2(conv_b[l]), wg_lru, row2(lru_ba[l]), row2(lru_bx[l]), row2(lru_lambda[l]))
        gla_args = (wal, row2(gla_b_alpha[l]), row2(gla_norm[l]))
        final = l == depth - 1

        qb, k_slab, v_slab, kb, vb, lru_in, gla_in = _inproj(
            xp, mod_p[l], row2(norm_mix[l]), w_mix, per_token=False, tok_per_seq=seq, tm=tm_p,
            kv_slab=(l, depth, k_slab, v_slab))
        y_sb = _sb_prompt(qb, kb, vb, sb_bias[l], batch=bp, seq=seq)
        y_lru, cv1, h1 = _lru(lru_in, zeros_conv, zeros_h, *lru_args, batch=bp, tok_per_seq=seq,
                              tc=tc_p, n_valid=tc_p, out_dtype=BF16)
        y_gla, s1 = _gla(gla_in, zeros_s, *gla_args, batch=bp, tok_per_seq=seq, c=c_gla,
                         n_valid=c_gla, out_dtype=BF16)
        xp = _merge(xp, mod_p[l], row2(norm_mix[l]), y_sb, y_lru, y_gla, w_gates, wbr, wout,
                    per_token=False, tok_per_seq=seq, tm=tm_p)
        xp = _ffn(xp, mod_p[l], row2(norm_ffn[l]), wgate, wup, wdown, row2(norm_final),
                  per_token=False, tok_per_seq=seq, tm=tm_p, final=final)
        outs["cvp"].append(cv1)
        outs["hp"].append(h1.reshape(bp, LRU_W))
        outs["sp"].append(s1.reshape(bp, GLA_H, GLA_DK, GLA_DV))

        qb, k, v, _, _, lru_in, gla_in = _inproj(
            xs, mod_s[l], row2(norm_mix[l]), w_mix, per_token=True, tok_per_seq=tsp, tm=tm_s)
        q4 = qb.reshape(bs, tsp, n_heads, SB_DH)[:, :n_new]
        eye = jnp.eye(n_heads, dtype=BF16)
        qbd = (q4[:, :, None, :, :] * eye[None, None, :, :, None]).reshape(bs, n_new * n_heads, SB_W)
        bias_rows = jnp.broadcast_to(jnp.tile(sb_bias[l], n_new)[:, None], (n_new * n_heads, 2 * PAGE))
        y_sb = _sb_sample(qbd, k, v, bias_rows, ck, cv, page_table, l, n_new=n_new, group=group)
        y_lru, cv2, h2 = _lru(lru_in, state_conv[l], state_lru[l].reshape(bs, 1, LRU_W), *lru_args,
                              batch=bs, tok_per_seq=tsp, tc=tsp, n_valid=n_new, out_dtype=F32)
        gla_pad = jnp.pad(gla_in.reshape(bs, tsp, GLA_IN_W),
                          ((0, 0), (0, GLA_SAMPLE_C - tsp), (0, 0))).reshape(bs * GLA_SAMPLE_C, GLA_IN_W)
        y_gla, s2 = _gla(gla_pad, state_gla[l].reshape(bs, GLA_KW, GLA_DV), *gla_args, batch=bs,
                         tok_per_seq=GLA_SAMPLE_C, c=GLA_SAMPLE_C, n_valid=n_new, out_dtype=F32)
        y_gla = y_gla.reshape(bs, GLA_SAMPLE_C, GLA_VW)[:, :tsp].reshape(bs * tsp, GLA_VW)
        xs = _merge(xs, mod_s[l], row2(norm_mix[l]), y_sb, y_lru, y_gla, w_gates, wbr, wout,
                    per_token=True, tok_per_seq=tsp, tm=tm_s)
        xs = _ffn(xs, mod_s[l], row2(norm_ffn[l]), wgate, wup, wdown, row2(norm_final),
                  per_token=True, tok_per_seq=tsp, tm=tm_s, final=final)
        outs["ks"].append(k.reshape(bs, tsp, n_heads, SB_DH)[:, :n_new])
        outs["vs"].append(v.reshape(bs, tsp, n_heads, SB_DH)[:, :n_new])
        outs["cvs"].append(cv2)
        outs["hs"].append(h2.reshape(bs, LRU_W))
        outs["ss"].append(s2.reshape(bs, GLA_H, GLA_DK, GLA_DV))

    y_prompt = xp.reshape(bp, seq, d)
    y_sample = xs.reshape(bs, tsp, d)[:, :n_new]
    st = lambda name: jnp.stack(outs[name])
    rows = lambda slab: jnp.transpose(slab.reshape(depth, bp, n_heads, SB_DH, seq), (0, 1, 4, 2, 3))
    return (y_prompt, y_sample, rows(k_slab), rows(v_slab), st("cvp"), st("hp"), st("sp"),
            st("ks"), st("vs"), st("cvs"), st("hs"), st("ss"))
```

```python
import functools

import numpy as np
import jax
import jax.numpy as jnp
from jax import lax
from jax.experimental import pallas as pl
from jax.experimental.pallas import tpu as pltpu

F32 = jnp.float32
BF16 = jnp.bfloat16

SB_DH = 64
SB_W = 512
LRU_W = 512
LRU_BLK = 64
CONV_TAPS = 4
LRU_SCALE = 8.0
GLA_H = 4
GLA_DK = 64
GLA_DV = 128
GLA_KW = GLA_H * GLA_DK
GLA_VW = GLA_H * GLA_DV
GLA_RANK = 16
GLA_TAU = 16.0
EPS = 1e-6
PAGE = 128
SB_TILE = 256

LANES = 128
SUBLANES = 8
VMEM_LIMIT = 56 * 1024 * 1024

SAMPLE_PAD_T = 8
GLA_SAMPLE_C = 32
GLA_IN_W = GLA_KW * 2 + GLA_VW * 2 + LANES
MIX_W = 3 * SB_W + 2 * LRU_W + GLA_IN_W


def _params(sem):
    return pltpu.CompilerParams(dimension_semantics=sem, vmem_limit_bytes=VMEM_LIMIT)


def _softplus(x):
    return jnp.maximum(x, 0.0) + jnp.log1p(jnp.exp(-jnp.abs(x)))


def _log_sigmoid(x):
    return jnp.minimum(x, 0.0) - jnp.log1p(jnp.exp(-jnp.abs(x)))


def _silu(x):
    return x * jax.nn.sigmoid(x)


def _gelu_tanh(x):
    return 0.5 * x * (1.0 + jnp.tanh(0.7978845608028654 * (x + 0.044715 * (x * x * x))))


def _split_bf16(x):
    hi = x.astype(BF16)
    lo = (x - hi.astype(F32)).astype(BF16)
    return hi, lo


def _mod_row(mod_ref, idx, per_token):
    if per_token:
        return mod_ref[idx]
    return mod_ref[idx:idx + 1, :]


def _norm_mod(x, g, shift, scale):
    xn = x * lax.rsqrt(jnp.mean(x * x, axis=-1, keepdims=True) + EPS) * g
    return xn * (1.0 + scale) + shift


def _mod_spec(per_token, tm, d, tok_per_seq):
    if per_token:
        return pl.BlockSpec((6, tm, d), lambda i, *_: (0, i, 0))
    blocks_per_seq = tok_per_seq // tm
    return pl.BlockSpec((None, 6, d), lambda i, *_: (i // blocks_per_seq, 0, 0))


def _ada_kernel(c_ref, w_ref, b_ref, o_ref):
    c = c_ref[...]
    s = _silu(c).astype(BF16)
    o_ref[...] = jnp.dot(s, w_ref[...].astype(BF16), preferred_element_type=F32) + b_ref[...]


def _ada(c_all, w_ada, b_ada):
    depth, d, e = w_ada.shape
    n = c_all.shape[0]
    tn = 1536 if e % 1536 == 0 else e
    return pl.pallas_call(
        _ada_kernel,
        grid=(depth, e // tn),
        in_specs=[
            pl.BlockSpec((n, d), lambda l, j: (0, 0)),
            pl.BlockSpec((None, d, tn), lambda l, j: (l, 0, j)),
            pl.BlockSpec((None, 1, tn), lambda l, j: (l, 0, j)),
        ],
        out_specs=pl.BlockSpec((None, n, tn), lambda l, j: (l, 0, j)),
        out_shape=jax.ShapeDtypeStruct((depth, n, e), F32),
        compiler_params=_params(("parallel", "parallel")),
    )(c_all, w_ada, b_ada.reshape(depth, 1, e))


def _inproj_kernel(x_ref, mod_ref, g_ref, w_ref, *rest, per_token, kv_slab):
    q_ref, k_ref, v_ref, kb_ref, vb_ref, lru_ref, gla_ref = rest[-7:]
    h = _norm_mod(x_ref[...], g_ref[...], _mod_row(mod_ref, 0, per_token),
                  _mod_row(mod_ref, 1, per_token)).astype(BF16)

    def seg(a, b):
        return jnp.dot(h, w_ref[:, a:b], preferred_element_type=F32)

    q = seg(0, SB_W) * (SB_DH ** -0.5)
    k = seg(SB_W, 2 * SB_W)
    v = seg(2 * SB_W, 3 * SB_W)
    if kv_slab:
        perm, perm_t = rest[0][...], rest[1][...]
        q_ref[...] = q.T.astype(BF16)
        k_ref[...] = k.T
        v_t = v.T
        v_ref[...] = v_t
        kb = k.astype(BF16)
        vb = v.astype(BF16)
        for t in range(k.shape[0] // SB_TILE):
            sl = slice(t * SB_TILE, (t + 1) * SB_TILE)
            kb_ref[sl, :] = jnp.dot(perm, kb[sl, :], preferred_element_type=F32).astype(BF16)
            vb_ref[t] = lax.dot_general(vb[sl, :], perm_t, (((0,), (0,)), ((), ())),
                                        preferred_element_type=F32).astype(BF16)
    else:
        q_ref[...] = q.astype(BF16)
        k_ref[...] = k
        kb_ref[...] = k.astype(BF16)
        v_ref[...] = v
        vb_ref[...] = v.astype(BF16)
    o = 3 * SB_W
    lru_ref[...] = seg(o, o + 2 * LRU_W)
    o += 2 * LRU_W
    gla_ref[...] = seg(o, o + GLA_IN_W)


def _key_permutation():
    groups = SB_TILE // SUBLANES
    p = np.zeros((SB_TILE, SB_TILE), np.float32)
    for r in range(groups):
        for sub in range(SUBLANES):
            p[SUBLANES * r + sub, groups * sub + r] = 1.0
    return jnp.asarray(p, dtype=BF16)


def _inproj(x, mod, g, w_mix, *, per_token, tok_per_seq, tm, kv_slab=None):
    n, d = x.shape
    row = lambda w: pl.BlockSpec((tm, w), lambda i: (i, 0))
    outs = [(SB_W, BF16), (SB_W, F32), (SB_W, F32), (SB_W, BF16), (SB_W, BF16),
            (2 * LRU_W, F32), (GLA_IN_W, F32)]
    out_specs = [row(w) for w, _ in outs]
    out_shape = [jax.ShapeDtypeStruct((n, w), dt) for w, dt in outs]
    extra_in, extra_specs, aliases = [], [], {}
    if kv_slab is not None:
        layer, depth, k_slab, v_slab = kv_slab
        bps = tok_per_seq // tm
        batch = n // tok_per_seq
        slab_spec = pl.BlockSpec((None, None, SB_W, tm), lambda i: (layer, i // bps, 0, i % bps))
        slab_shape = jax.ShapeDtypeStruct((depth, batch, SB_W, tok_per_seq), F32)
        out_specs[1] = out_specs[2] = slab_spec
        out_shape[1] = out_shape[2] = slab_shape
        out_specs[0] = pl.BlockSpec((None, SB_W, tm), lambda i: (i // bps, 0, i % bps))
        out_shape[0] = jax.ShapeDtypeStruct((batch, SB_W, tok_per_seq), BF16)
        tpb = tm // SB_TILE
        out_specs[4] = pl.BlockSpec((None, tpb, SB_W, SB_TILE), lambda i: (i // bps, i % bps, 0, 0))
        out_shape[4] = jax.ShapeDtypeStruct((batch, tok_per_seq // SB_TILE, SB_W, SB_TILE), BF16)
        perm = _key_permutation()
        extra_in = [perm, perm.T]
        extra_specs = [pl.BlockSpec((SB_TILE, SB_TILE), lambda i: (0, 0))] * 2
        if k_slab is not None:
            extra_in += [k_slab, v_slab]
            extra_specs += [pl.BlockSpec(memory_space=pl.ANY)] * 2
            aliases = {6: 1, 7: 2}
    return pl.pallas_call(
        functools.partial(_inproj_kernel, per_token=per_token, kv_slab=kv_slab is not None),
        grid=(n // tm,),
        in_specs=[
            row(d),
            _mod_spec(per_token, tm, d, tok_per_seq),
            pl.BlockSpec((1, d), lambda i: (0, 0)),
            pl.BlockSpec((d, MIX_W), lambda i: (0, 0)),
        ] + extra_specs,
        out_specs=out_specs,
        out_shape=out_shape,
        input_output_aliases=aliases,
        compiler_params=_params(("parallel",)),
    )(x, mod, g, w_mix, *extra_in)


def _sb_logs(z):
    t = jnp.log(1.0 + jnp.exp(-jnp.abs(z)))
    m = jnp.minimum(z, 0.0)
    return m - t, (m - z) - t


def _sb_logs_neg(z):
    neg_log_keep = jnp.maximum(z, 0.0) + jnp.log(1.0 + jnp.exp(-jnp.abs(z)))
    return z - neg_log_keep, neg_log_keep


def _suffix_lhs(log_keep, m2):
    if m2.shape[0] == log_keep.shape[1]:
        return log_keep.astype(BF16)
    return jnp.concatenate(_split_bf16(log_keep), axis=1)


def _sb_tile(qh, kblk, vblk, bias, m2, carry, acc, valid):
    z = lax.dot_general(qh, kblk, (((1,), (1,)), ((), ())), preferred_element_type=F32) + bias
    log_beta, log_keep = _sb_logs(z)
    if valid is not None:
        log_keep = jnp.where(valid, log_keep, 0.0)
    sfx = jnp.dot(_suffix_lhs(log_keep, m2), m2, preferred_element_type=F32)
    a = jnp.exp(log_beta + sfx + carry)
    if valid is not None:
        a = jnp.where(valid, a, 0.0)
    carry = carry + jnp.sum(log_keep, axis=1, keepdims=True)
    acc = acc + jnp.dot(a.astype(BF16), vblk, preferred_element_type=F32)
    return carry, acc


def _sb_tiles_keys_major(q2t, k_rows, vt_cols, bias_row, top_valid):
    n_tiles = k_rows.shape[0] // SB_TILE
    groups = SB_TILE // SUBLANES
    width = q2t.shape[1]
    zt = jnp.dot(k_rows, q2t, preferred_element_type=F32) + bias_row
    log_beta, log_keep = _sb_logs_neg(zt)
    sub = lax.broadcasted_iota(jnp.int32, (SUBLANES, width), 0)
    carry = jnp.zeros((SUBLANES, width), F32)
    a_tiles = [None] * n_tiles
    for t in reversed(range(n_tiles)):
        top = t == n_tiles - 1
        rows = [slice(t * SB_TILE + SUBLANES * r, t * SB_TILE + SUBLANES * (r + 1)) for r in range(groups)]
        masks = [top_valid[SUBLANES * r:SUBLANES * (r + 1)] for r in range(groups)] if top else None
        below = [None] * groups
        run = jnp.zeros((SUBLANES, width), F32)
        for r in reversed(range(groups)):
            below[r] = run
            lk = log_keep[rows[r]]
            run = run + (jnp.where(masks[r], lk, 0.0) if top else lk)
        inc = run
        for d in (1, 2, 4):
            inc = inc + jnp.where(sub < SUBLANES - d, pltpu.roll(inc, SUBLANES - d, axis=0), 0.0)
        later = (inc - run) + carry
        carry = carry + jnp.broadcast_to(inc[0:1], (SUBLANES, width))
        a_rows = [jnp.exp(log_beta[rows[r]] - below[r] - later) for r in range(groups)]
        if top:
            a_rows = [jnp.where(masks[r], a_rows[r], 0.0) for r in range(groups)]
        a_tiles[t] = jnp.concatenate(a_rows, axis=0)
    a = a_tiles[0] if n_tiles == 1 else jnp.concatenate(a_tiles, axis=0)
    return jnp.dot(vt_cols, a.astype(BF16), preferred_element_type=F32)


def _sb_prompt_kernel(bias_ref, qt_ref, k_ref, vt_ref, o_ref, *, tq, nq, qb):
    hp = pl.program_id(1)
    gi = pl.program_id(2)
    qt = qt_ref[...].astype(F32)
    feat = lax.broadcasted_iota(jnp.int32, (LANES, tq), 0)
    lane1 = lax.broadcasted_iota(jnp.int32, (1, 2 * tq), 1)
    bias_row = jnp.where(lane1 < tq, bias_ref[2 * hp], bias_ref[2 * hp + 1])
    row = lax.broadcasted_iota(jnp.int32, (tq, 2 * tq), 0)
    lane = lax.broadcasted_iota(jnp.int32, (tq, 2 * tq), 1)
    key = (row & (SUBLANES - 1)) * (tq // SUBLANES) + (row >> 3)
    diag_valid = key < (lane & (tq - 1))

    def blocks(g):
        outs = []
        for s in range(qb):
            n = g * qb + s + 1
            q = qt[:, s * tq:(s + 1) * tq]
            q2t = jnp.concatenate([jnp.where(feat < SB_DH, q, 0.0), jnp.where(feat >= SB_DH, q, 0.0)],
                                  axis=1).astype(BF16)
            vt = vt_ref[0] if n == 1 else jnp.concatenate([vt_ref[t] for t in range(n)], axis=1)
            acc = _sb_tiles_keys_major(q2t, k_ref[0:n * tq, :], vt, bias_row, diag_valid)
            outs.append(jnp.concatenate([acc[:SB_DH, :tq], acc[SB_DH:, tq:]], axis=0).T)
        return outs[0] if qb == 1 else jnp.concatenate(outs, axis=0)

    out = lax.switch(gi, [functools.partial(blocks, g) for g in range(nq // qb)])
    o_ref[...] = out.astype(o_ref.dtype)


def _suffix_matrix(t, parts):
    m = (np.arange(t)[:, None] > np.arange(t)[None, :]).astype(np.float32)
    return jnp.asarray(np.concatenate([m] * parts, axis=0), dtype=BF16)


def _sb_prompt(qt, kperm, vt_tiles, bias, *, batch, seq):
    tq = SB_TILE
    nq = seq // tq
    assert nq <= 16, "one unrolled body per query block: sequence too long for this kernel"
    qb = _pick_tile(nq, 8)
    steps = nq // qb
    return pl.pallas_call(
        functools.partial(_sb_prompt_kernel, tq=tq, nq=nq, qb=qb),
        grid=(batch, SB_W // LANES, steps),
        in_specs=[
            pl.BlockSpec(memory_space=pltpu.SMEM),
            pl.BlockSpec((None, LANES, qb * tq), lambda b, h, i: (b, h, i)),
            pl.BlockSpec((seq, LANES), lambda b, h, i: (b, h)),
            pl.BlockSpec((None, nq, LANES, tq), lambda b, h, i: (b, 0, h, 0)),
        ],
        out_specs=pl.BlockSpec((qb * tq, LANES), lambda b, h, i: (b * steps + i, h)),
        out_shape=jax.ShapeDtypeStruct((batch * seq, SB_W), BF16),
        compiler_params=_params(("parallel", "parallel", "arbitrary")),
    )(bias, qt, kperm, vt_tiles)


def _sb_sample_kernel(pt_ref, qbd_ref, kn_ref, vn_ref, bias_ref, m2_ref, *refs, group, n_new, n_heads):
    del pt_ref
    kp = refs[:group]
    vp = refs[group:2 * group]
    o_ref = refs[2 * group]
    carry_ref, acc_ref = refs[2 * group + 1:]
    p = pl.program_id(1)
    qbd = qbd_ref[...]
    bias = bias_ref[...]
    m2 = m2_ref[...]
    rows = n_new * n_heads
    width = 2 * PAGE
    n_pair = group // 2

    @pl.when(p == 0)
    def _():
        pad = jnp.zeros((width - SAMPLE_PAD_T, SB_W), F32)
        kn = jnp.concatenate([kn_ref[...], pad], axis=0).astype(BF16)
        vn = jnp.concatenate([vn_ref[...], pad], axis=0).astype(BF16)
        row = lax.broadcasted_iota(jnp.int32, (rows, width), 0)
        col = lax.broadcasted_iota(jnp.int32, (rows, width), 1)
        valid = col < row // n_heads
        carry, acc = _sb_tile(qbd, kn, vn, bias, m2, jnp.zeros((rows, 1), F32),
                              jnp.zeros((rows, SB_W), F32), valid)
        carry_ref[...] = carry
        acc_ref[...] = acc

    def pair(page_refs, j):
        low = page_refs[group - 1 - 2 * j][...]
        high = page_refs[group - 2 - 2 * j][...]
        return jnp.concatenate([low, high], axis=1).astype(BF16)

    logs = [_sb_logs(jnp.dot(qbd, pair(kp, j), preferred_element_type=F32) + bias) for j in range(n_pair)]
    split = [jnp.concatenate(_split_bf16(lk), axis=1) for _, lk in logs]
    sfx = jnp.dot(jnp.concatenate(split, axis=0), m2, preferred_element_type=F32)
    carry = carry_ref[...]
    carries = [None] * n_pair
    for j in reversed(range(n_pair)):
        carries[j] = carry
        carry = carry + jnp.sum(logs[j][1], axis=1, keepdims=True)
    acc = acc_ref[...]
    for j in range(n_pair):
        a = jnp.exp(logs[j][0] + sfx[j * rows:(j + 1) * rows] + carries[j]).astype(BF16)
        acc = acc + lax.dot_general(a, pair(vp, j), (((1,), (1,)), ((), ())), preferred_element_type=F32)
    carry_ref[...] = carry
    acc_ref[...] = acc

    @pl.when(p == pl.num_programs(1) - 1)
    def _():
        row = lax.broadcasted_iota(jnp.int32, (rows, SB_W), 0)
        col = lax.broadcasted_iota(jnp.int32, (rows, SB_W), 1)
        own = jnp.where(col // SB_DH == row % n_heads, acc, 0.0)
        o_ref[...] = jnp.zeros(o_ref.shape, o_ref.dtype)
        for qq in range(n_new):
            o_ref[qq:qq + 1, :] = jnp.sum(own[qq * n_heads:(qq + 1) * n_heads, :], axis=0, keepdims=True)


def _sb_sample(qbd, k_new, v_new, bias_rows, cache_k, cache_v, page_table, layer, *, n_new, group):
    bs, n_pages = page_table.shape
    n_heads = SB_W // SB_DH
    rows = n_new * n_heads
    steps = n_pages // group

    def page_spec(i):
        def index(b, p, pt):
            return (layer, pt[b, n_pages - 1 - (p * group + i)], 0, 0)
        return pl.BlockSpec((None, None, SB_W, PAGE), index)

    grid_spec = pltpu.PrefetchScalarGridSpec(
        num_scalar_prefetch=1,
        grid=(bs, steps),
        in_specs=[
            pl.BlockSpec((None, rows, SB_W), lambda b, p, pt: (b, 0, 0)),
            pl.BlockSpec((SAMPLE_PAD_T, SB_W), lambda b, p, pt: (b, 0)),
            pl.BlockSpec((SAMPLE_PAD_T, SB_W), lambda b, p, pt: (b, 0)),
            pl.BlockSpec((rows, 2 * PAGE), lambda b, p, pt: (0, 0)),
            pl.BlockSpec((4 * PAGE, 2 * PAGE), lambda b, p, pt: (0, 0)),
        ] + [page_spec(i) for i in range(group)] * 2,
        out_specs=pl.BlockSpec((SAMPLE_PAD_T, SB_W), lambda b, p, pt: (b, 0)),
        scratch_shapes=[pltpu.VMEM((rows, 1), F32), pltpu.VMEM((rows, SB_W), F32)],
    )
    return pl.pallas_call(
        functools.partial(_sb_sample_kernel, group=group, n_new=n_new, n_heads=n_heads),
        grid_spec=grid_spec,
        out_shape=jax.ShapeDtypeStruct((bs * SAMPLE_PAD_T, SB_W), F32),
        compiler_params=_params(("parallel", "arbitrary")),
    )(page_table, qbd, k_new, v_new, bias_rows, _suffix_matrix(2 * PAGE, 2),
      *([cache_k] * group), *([cache_v] * group))


def _lru_kernel(x_ref, conv0_ref, h0_ref, cw_ref, cb_ref, wg_ref, ba_ref, bx_ref, lam_ref,
                y_ref, conv_ref, hl_ref, xpad, a_s, b_s, hs_s, hcar, *, tc, n_valid):
    ci = pl.program_id(1)
    head = SUBLANES
    tail = CONV_TAPS - 1

    @pl.when(ci == 0)
    def _():
        xpad[head - tail:head, :] = conv0_ref[...]
        hcar[...] = h0_ref[...]

    xpad[head:head + tc, :] = x_ref[:, 0:LRU_W]
    xc = cb_ref[...] + xpad[head - tail:head - tail + tc, :] * cw_ref[0:1, :]
    for w in range(1, CONV_TAPS):
        xc = xc + xpad[head - tail + w:head - tail + w + tc, :] * cw_ref[w:w + 1, :]
    xcb = xc.astype(BF16)
    neg_sp = -LRU_SCALE * _softplus(-lam_ref[...])
    for j in range(LRU_W // LANES):
        sl = slice(j * LANES, (j + 1) * LANES)
        g = jnp.dot(xcb[:, sl], wg_ref[j], preferred_element_type=F32)
        r = jax.nn.sigmoid(g[:, :LANES] + ba_ref[:, sl])
        gate_in = jax.nn.sigmoid(g[:, LANES:] + bx_ref[:, sl])
        log_a = neg_sp[:, sl] * r
        a = jnp.exp(log_a)
        mult = jnp.sqrt(-jnp.tanh(log_a) * (a * a + 1.0))
        a_s[:, sl] = a
        b_s[:, sl] = mult * (gate_in * xc[:, sl])

    rowi = lax.broadcasted_iota(jnp.int32, (SUBLANES, LRU_W), 0)

    def tile_scan(g, h):
        r0 = pl.multiple_of(g * SUBLANES, SUBLANES)
        a = a_s[pl.ds(r0, SUBLANES), :]
        b = b_s[pl.ds(r0, SUBLANES), :]
        for d in (1, 2, 4):
            keep = rowi >= d
            b = jnp.where(keep, a * pltpu.roll(b, d, axis=0) + b, b)
            a = jnp.where(keep, a * pltpu.roll(a, d, axis=0), a)
        hs = a * h + b
        hs_s[pl.ds(r0, SUBLANES), :] = hs
        return hs[SUBLANES - 1:SUBLANES, :]

    hcar[...] = lax.fori_loop(0, tc // SUBLANES, tile_scan, hcar[...])
    y_ref[...] = (hs_s[...] * _gelu_tanh(x_ref[:, LRU_W:2 * LRU_W])).astype(y_ref.dtype)

    new_tail = xpad[head + n_valid - tail:head + n_valid, :]
    conv_ref[...] = new_tail
    hl_ref[...] = hs_s[n_valid - 1:n_valid, :]
    xpad[head - tail:head, :] = new_tail


def _lru(lru_in, conv0, h0, cw, cb, wg, ba, bx, lam, *, batch, tok_per_seq, tc, n_valid, out_dtype):
    n = lru_in.shape[0]
    nc = tok_per_seq // tc
    vec = pl.BlockSpec((1, LRU_W), lambda b, c: (0, 0))
    tail = CONV_TAPS - 1
    return pl.pallas_call(
        functools.partial(_lru_kernel, tc=tc, n_valid=n_valid),
        grid=(batch, nc),
        in_specs=[
            pl.BlockSpec((tc, 2 * LRU_W), lambda b, c: (b * nc + c, 0)),
            pl.BlockSpec((None, tail, LRU_W), lambda b, c: (b, 0, 0)),
            pl.BlockSpec((None, 1, LRU_W), lambda b, c: (b, 0, 0)),
            pl.BlockSpec((CONV_TAPS, LRU_W), lambda b, c: (0, 0)),
            vec,
            pl.BlockSpec((LRU_W // LANES, LANES, 2 * LANES), lambda b, c: (0, 0, 0)),
            vec, vec, vec,
        ],
        out_specs=[
            pl.BlockSpec((tc, LRU_W), lambda b, c: (b * nc + c, 0)),
            pl.BlockSpec((None, tail, LRU_W), lambda b, c: (b, 0, 0)),
            pl.BlockSpec((None, 1, LRU_W), lambda b, c: (b, 0, 0)),
        ],
        out_shape=[
            jax.ShapeDtypeStruct((n, LRU_W), out_dtype),
            jax.ShapeDtypeStruct((batch, tail, LRU_W), F32),
            jax.ShapeDtypeStruct((batch, 1, LRU_W), F32),
        ],
        scratch_shapes=[
            pltpu.VMEM((SUBLANES + tc, LRU_W), F32),
            pltpu.VMEM((tc, LRU_W), F32),
            pltpu.VMEM((tc, LRU_W), F32),
            pltpu.VMEM((tc, LRU_W), F32),
            pltpu.VMEM((1, LRU_W), F32),
        ],
        compiler_params=_params(("parallel", "arbitrary")),
    )(lru_in, conv0, h0, cw, cb, wg, ba, bx, lam)


def _gla_kernel(x_ref, s0_ref, wal_ref, bal_ref, gn_ref, tri_ref, y_ref, sout_ref, st_ref, *, c, n_valid, nseq):
    ci = pl.program_id(1)
    seqs = range(nseq)
    nt = (((1,), (1,)), ((), ()))

    @pl.when(ci == 0)
    def _():
        for s in seqs:
            st_ref[s] = s0_ref[s].T

    xs = [x_ref[s] for s in seqs]
    o_v = 2 * GLA_KW
    o_go = o_v + GLA_VW
    o_lr = o_go + GLA_VW
    glr = jnp.concatenate([x[:, o_lr:] for x in xs], axis=0).astype(BF16)
    xl = jnp.dot(glr, wal_ref[...], preferred_element_type=F32) + bal_ref[...]
    la = _log_sigmoid(xl) * (1.0 / GLA_TAU)
    ks = [x[:, GLA_KW:o_v] for x in xs]
    las = [la[s * c:(s + 1) * c] for s in seqs]
    if n_valid < c:
        rowi = lax.broadcasted_iota(jnp.int32, (c, GLA_KW), 0)
        las = [jnp.where(rowi < n_valid, l, 0.0) for l in las]
        ks = [jnp.where(rowi < n_valid, k, 0.0) for k in ks]
    parts = [_split_bf16(l) for l in las]
    tri = tri_ref[...]
    g_all = (jnp.dot(tri, jnp.concatenate([p[0] for p in parts], axis=1), preferred_element_type=F32)
             + jnp.dot(tri, jnp.concatenate([p[1] for p in parts], axis=1), preferred_element_type=F32))

    lane_k = lax.broadcasted_iota(jnp.int32, (1, GLA_KW), 1) // GLA_DK
    row = lax.broadcasted_iota(jnp.int32, (c, GLA_H * c), 0)
    col = lax.broadcasted_iota(jnp.int32, (c, GLA_H * c), 1)
    causal = col % c <= row
    col_h = col // c

    def by_head(a):
        return jnp.concatenate([jnp.where(lane_k == h, a, 0.0) for h in range(GLA_H)], axis=0)

    q_ins, q_rels, kbd_rels, kbd_decs, decays = [], [], [], [], []
    for s in seqs:
        g = g_all[:, s * GLA_KW:(s + 1) * GLA_KW]
        g_last = g[c - 1:c, :]
        g_mid = g[c // 2:c // 2 + 1, :]
        q = xs[s][:, 0:GLA_KW] * (GLA_DK ** -0.5)
        q_ins.append(q * jnp.exp(g))
        q_rels.append((q * jnp.exp(g - g_mid)).astype(BF16))
        kbd_rels.append(by_head(ks[s] * jnp.exp(g_mid - g)).astype(BF16))
        kbd_decs.append(by_head(ks[s] * jnp.exp(g_last - g)).astype(BF16))
        decays.append(jnp.exp(g_last))

    atts = [lax.dot_general(q_rels[s], kbd_rels[s], nt, preferred_element_type=F32) for s in seqs]
    v_ts, w_ts, lhss = [], [], []
    for s in seqs:
        v = xs[s][:, o_v:o_go]
        v_t = jnp.concatenate([v[:, h * GLA_DV:(h + 1) * GLA_DV] for h in range(GLA_H)], axis=0).T
        v_ts.append(v_t)
        w_ts.append(jnp.concatenate([st_ref[s], v_t], axis=1).astype(BF16))
        att = jnp.where(causal, atts[s], 0.0)
        lhss.append(jnp.concatenate(
            [jnp.concatenate([jnp.where(lane_k == h, q_ins[s], 0.0), jnp.where(col_h == h, att, 0.0)], axis=1)
             for h in range(GLA_H)], axis=0).astype(BF16))
    ress = [lax.dot_general(lhss[s], w_ts[s], nt, preferred_element_type=F32) for s in seqs]
    upds = [jnp.dot(v_ts[s].astype(BF16), kbd_decs[s], preferred_element_type=F32) for s in seqs]

    for s in seqs:
        st_new = st_ref[s] * decays[s] + upds[s]
        st_ref[s] = st_new
        outs = []
        for h in range(GLA_H):
            oh = ress[s][h * c:(h + 1) * c, :]
            oh = oh * lax.rsqrt(jnp.mean(oh * oh, axis=-1, keepdims=True) + EPS)
            outs.append(oh * gn_ref[:, h * GLA_DV:(h + 1) * GLA_DV])
        go = xs[s][:, o_go:o_lr]
        y_ref[s] = (jnp.concatenate(outs, axis=1) * _silu(go)).astype(y_ref.dtype)

    @pl.when(ci == pl.num_programs(1) - 1)
    def _():
        for s in seqs:
            sout_ref[s] = st_ref[s].T


def _gla(gla_in, s0, wal, bal, gn, *, batch, tok_per_seq, c, n_valid, out_dtype):
    nc = tok_per_seq // c
    nseq = _pick_tile(batch, 8)
    tri = jnp.asarray((np.arange(c)[:, None] >= np.arange(c)[None, :]).astype(np.float32), dtype=BF16)
    state = pl.BlockSpec((nseq, GLA_KW, GLA_DV), lambda b, i: (b, 0, 0))
    y, s_out = pl.pallas_call(
        functools.partial(_gla_kernel, c=c, n_valid=n_valid, nseq=nseq),
        grid=(batch // nseq, nc),
        in_specs=[
            pl.BlockSpec((nseq, c, GLA_IN_W), lambda b, i: (b, i, 0)),
            state,
            pl.BlockSpec((LANES, GLA_KW), lambda b, i: (0, 0)),
            pl.BlockSpec((1, GLA_KW), lambda b, i: (0, 0)),
            pl.BlockSpec((1, GLA_VW), lambda b, i: (0, 0)),
            pl.BlockSpec((c, c), lambda b, i: (0, 0)),
        ],
        out_specs=[pl.BlockSpec((nseq, c, GLA_VW), lambda b, i: (b, i, 0)), state],
        out_shape=[
            jax.ShapeDtypeStruct((batch, tok_per_seq, GLA_VW), out_dtype),
            jax.ShapeDtypeStruct((batch, GLA_KW, GLA_DV), F32),
        ],
        scratch_shapes=[pltpu.VMEM((nseq, GLA_DV, GLA_KW), F32)],
        compiler_params=_params(("parallel", "arbitrary")),
    )(gla_in.reshape(batch, tok_per_seq, GLA_IN_W), s0, wal, bal, gn, tri)
    return y.reshape(batch * tok_per_seq, GLA_VW), s_out


def _merge_kernel(x_ref, mod_ref, g_ref, ysb_ref, ylru_ref, ygla_ref, wg_ref, wbr_ref, wout_ref,
                  o_ref, *, per_token):
    x = x_ref[...]
    d = x.shape[1]
    h = _norm_mod(x, g_ref[...], _mod_row(mod_ref, 0, per_token),
                  _mod_row(mod_ref, 1, per_token)).astype(BF16)
    merged = None
    for b, y_ref in enumerate((ysb_ref, ylru_ref, ygla_ref)):
        gate = jax.nn.sigmoid(jnp.dot(h, wg_ref[:, b * d:(b + 1) * d], preferred_element_type=F32))
        br = jnp.dot(y_ref[...].astype(BF16), wbr_ref[b], preferred_element_type=F32)
        merged = gate * br if merged is None else merged + gate * br
    out = jnp.dot(merged.astype(BF16), wout_ref[...], preferred_element_type=F32)
    o_ref[...] = x + _mod_row(mod_ref, 2, per_token) * out


def _merge(x, mod, g, ysb, ylru, ygla, wg, wbr, wout, *, per_token, tok_per_seq, tm):
    n, d = x.shape
    row = lambda w: pl.BlockSpec((tm, w), lambda i: (i, 0))
    return pl.pallas_call(
        functools.partial(_merge_kernel, per_token=per_token),
        grid=(n // tm,),
        in_specs=[
            row(d),
            _mod_spec(per_token, tm, d, tok_per_seq),
            pl.BlockSpec((1, d), lambda i: (0, 0)),
            row(SB_W), row(LRU_W), row(GLA_VW),
            pl.BlockSpec((d, 3 * d), lambda i: (0, 0)),
            pl.BlockSpec((3, SB_W, d), lambda i: (0, 0, 0)),
            pl.BlockSpec((d, d), lambda i: (0, 0)),
        ],
        out_specs=row(d),
        out_shape=jax.ShapeDtypeStruct((n, d), F32),
        compiler_params=_params(("parallel",)),
    )(x, mod, g, ysb, ylru, ygla, wg, wbr, wout)


def _ffn_kernel(x_ref, mod_ref, g_ref, wgate_ref, wup_ref, wdown_ref, gfin_ref, o_ref, h_s, acc_s,
                *, per_token, final):
    f = pl.program_id(1)

    @pl.when(f == 0)
    def _():
        h_s[...] = _norm_mod(x_ref[...], g_ref[...], _mod_row(mod_ref, 3, per_token),
                             _mod_row(mod_ref, 4, per_token)).astype(BF16)
        acc_s[...] = jnp.zeros(acc_s.shape, F32)

    h = h_s[...]
    ff = _silu(jnp.dot(h, wgate_ref[...], preferred_element_type=F32)) * jnp.dot(
        h, wup_ref[...], preferred_element_type=F32)
    acc_s[...] += jnp.dot(ff.astype(BF16), wdown_ref[...], preferred_element_type=F32)

    @pl.when(f == pl.num_programs(1) - 1)
    def _():
        x2 = x_ref[...] + _mod_row(mod_ref, 5, per_token) * acc_s[...]
        if final:
            x2 = x2 * lax.rsqrt(jnp.mean(x2 * x2, axis=-1, keepdims=True) + EPS) * gfin_ref[...]
        o_ref[...] = x2


def _ffn(x, mod, g, wgate, wup, wdown, gfin, *, per_token, tok_per_seq, tm, final):
    n, d = x.shape
    dff = wgate.shape[1]
    tf = dff
    once = dict(pipeline_mode=pl.Buffered(1))
    mod_spec = _mod_spec(per_token, tm, d, tok_per_seq)
    return pl.pallas_call(
        functools.partial(_ffn_kernel, per_token=per_token, final=final),
        grid=(n // tm, dff // tf),
        in_specs=[
            pl.BlockSpec((tm, d), lambda i, f: (i, 0)),
            mod_spec,
            pl.BlockSpec((1, d), lambda i, f: (0, 0)),
            pl.BlockSpec((d, tf), lambda i, f: (0, f), **once),
            pl.BlockSpec((d, tf), lambda i, f: (0, f), **once),
            pl.BlockSpec((tf, d), lambda i, f: (f, 0), **once),
            pl.BlockSpec((1, d), lambda i, f: (0, 0)),
        ],
        out_specs=pl.BlockSpec((tm, d), lambda i, f: (i, 0)),
        out_shape=jax.ShapeDtypeStruct((n, d), F32),
        scratch_shapes=[pltpu.VMEM((tm, d), BF16), pltpu.VMEM((tm, d), F32)],
        compiler_params=_params(("parallel", "arbitrary")),
    )(x, mod, g, wgate, wup, wdown, gfin)


def _split_w_in(w_in_l):
    d = w_in_l.shape[0]
    sizes = (SB_W, SB_W, SB_W, LRU_W, LRU_W, GLA_KW, GLA_KW, GLA_VW, GLA_VW, GLA_RANK, d, d, d)
    offs = np.concatenate([[0], np.cumsum(sizes)])
    seg = lambda i: w_in_l[:, offs[i]:offs[i + 1]]
    glr = jnp.pad(seg(9), ((0, 0), (0, LANES - GLA_RANK)))
    w_mix = jnp.concatenate([seg(i) for i in range(9)] + [glr], axis=1).astype(BF16)
    w_gates = jnp.concatenate([seg(10), seg(11), seg(12)], axis=1).astype(BF16)
    return w_mix, w_gates


def _lru_gate_tiles(wa, wx):
    nb = wa.shape[0]
    z = jnp.zeros((LRU_BLK, LRU_BLK), F32)

    def pair(w, j):
        return jnp.concatenate([jnp.concatenate([w[2 * j], z], axis=1),
                                jnp.concatenate([z, w[2 * j + 1]], axis=1)], axis=0)

    tiles = [jnp.concatenate([pair(wa, j), pair(wx, j)], axis=1) for j in range(nb // 2)]
    return jnp.stack(tiles).astype(BF16)


def _pick_tile(n, pref):
    t = pref
    while n % t:
        t //= 2
    return t


def kernel(x_prompt, x_sample, c_prompt, c_sample, cache_k, cache_v, page_table, state_conv, state_lru, state_gla, w_ada, b_ada, norm_mix, w_in, sb_bias, conv_w, conv_b, lru_wa, lru_ba, lru_wx, lru_bx, lru_lambda, gla_w_alpha, gla_b_alpha, gla_norm, w_br_sb, w_br_lru, w_br_gla, w_out, norm_ffn, w_gate, w_up, w_down, norm_final):
    bp, seq, d = x_prompt.shape
    bs, n_new, _ = x_sample.shape
    depth = w_ada.shape[0]
    n_heads = SB_W // SB_DH
    n_pool = cache_k.shape[1]
    tsp = SAMPLE_PAD_T

    xp = x_prompt.reshape(bp * seq, d)
    xs = jnp.pad(x_sample, ((0, 0), (0, tsp - n_new), (0, 0))).reshape(bs * tsp, d)
    ck = jnp.transpose(cache_k, (0, 1, 3, 4, 2)).reshape(depth, n_pool, SB_W, PAGE)
    cv = jnp.transpose(cache_v, (0, 1, 3, 4, 2)).reshape(depth, n_pool, SB_W, PAGE)

    mod = _ada(jnp.concatenate([c_prompt, c_sample], axis=0), w_ada, b_ada)
    mod = mod.reshape(depth, bp + bs, 6, d)
    mod_p = mod[:, :bp]
    mod_s = jnp.transpose(jnp.repeat(mod[:, bp:], tsp, axis=1), (0, 2, 1, 3))

    tm_p = _pick_tile(seq, 512)
    tm_s = bs * tsp
    tc_p = _pick_tile(seq, 512)
    c_gla = _pick_tile(seq, 64)
    group = _pick_tile(page_table.shape[1], 16)

    zeros_conv = jnp.zeros((bp, CONV_TAPS - 1, LRU_W), F32)
    zeros_h = jnp.zeros((bp, 1, LRU_W), F32)
    zeros_s = jnp.zeros((bp, GLA_KW, GLA_DV), F32)
    row2 = lambda a: a.reshape(1, -1)

    outs = {k: [] for k in ("cvp", "hp", "sp", "ks", "vs", "cvs", "hs", "ss")}
    k_slab = v_slab = None
    for l in range(depth):
        w_mix, w_gates = _split_w_in(w_in[l])
        wg_lru = _lru_gate_tiles(lru_wa[l], lru_wx[l])
        wal = jnp.pad(gla_w_alpha[l], ((0, LANES - GLA_RANK), (0, 0))).astype(BF16)
        wbr = jnp.stack([w_br_sb[l], w_br_lru[l], w_br_gla[l]]).astype(BF16)
        wout = w_out[l].astype(BF16)
        wgate, wup, wdown = w_gate[l].astype(BF16), w_up[l].astype(BF16), w_down[l].astype(BF16)
        lru_args = (conv_w[l], row2(conv_b[l]), wg_lru, row2(lru_ba[l]), row2(lru_bx[l]), row2(lru_lambda[l]))
        gla_args = (wal, row2(gla_b_alpha[l]), row2(gla_norm[l]))
        final = l == depth - 1

        qb, k_slab, v_slab, kb, vb, lru_in, gla_in = _inproj(
            xp, mod_p[l], row2(norm_mix[l]), w_mix, per_token=False, tok_per_seq=seq, tm=tm_p,
            kv_slab=(l, depth, k_slab, v_slab))
        y_sb = _sb_prompt(qb, kb, vb, sb_bias[l], batch=bp, seq=seq)
        y_lru, cv1, h1 = _lru(lru_in, zeros_conv, zeros_h, *lru_args, batch=bp, tok_per_seq=seq,
                              tc=tc_p, n_valid=tc_p, out_dtype=BF16)
        y_gla, s1 = _gla(gla_in, zeros_s, *gla_args, batch=bp, tok_per_seq=seq, c=c_gla,
                         n_valid=c_gla, out_dtype=BF16)
        xp = _merge(xp, mod_p[l], row2(norm_mix[l]), y_sb, y_lru, y_gla, w_gates, wbr, wout,
                    per_token=False, tok_per_seq=seq, tm=tm_p)
        xp = _ffn(xp, mod_p[l], row2(norm_ffn[l]), wgate, wup, wdown, row2(norm_final),
                  per_token=False, tok_per_seq=seq, tm=tm_p, final=final)
        outs["cvp"].append(cv1)
        outs["hp"].append(h1.reshape(bp, LRU_W))
        outs["sp"].append(s1.reshape(bp, GLA_H, GLA_DK, GLA_DV))

        qb, k, v, _, _, lru_in, gla_in = _inproj(
            xs, mod_s[l], row2(norm_mix[l]), w_mix, per_token=True, tok_per_seq=tsp, tm=tm_s)
        q4 = qb.reshape(bs, tsp, n_heads, SB_DH)[:, :n_new]
        eye = jnp.eye(n_heads, dtype=BF16)
        qbd = (q4[:, :, None, :, :] * eye[None, None, :, :, None]).reshape(bs, n_new * n_heads, SB_W)
        bias_rows = jnp.broadcast_to(jnp.tile(sb_bias[l], n_new)[:, None], (n_new * n_heads, 2 * PAGE))
        y_sb = _sb_sample(qbd, k, v, bias_rows, ck, cv, page_table, l, n_new=n_new, group=group)
        y_lru, cv2, h2 = _lru(lru_in, state_conv[l], state_lru[l].reshape(bs, 1, LRU_W), *lru_args,
                              batch=bs, tok_per_seq=tsp, tc=tsp, n_valid=n_new, out_dtype=F32)
        gla_pad = jnp.pad(gla_in.reshape(bs, tsp, GLA_IN_W),
                          ((0, 0), (0, GLA_SAMPLE_C - tsp), (0, 0))).reshape(bs * GLA_SAMPLE_C, GLA_IN_W)
        y_gla, s2 = _gla(gla_pad, state_gla[l].reshape(bs, GLA_KW, GLA_DV), *gla_args, batch=bs,
                         tok_per_seq=GLA_SAMPLE_C, c=GLA_SAMPLE_C, n_valid=n_new, out_dtype=F32)
        y_gla = y_gla.reshape(bs, GLA_SAMPLE_C, GLA_VW)[:, :tsp].reshape(bs * tsp, GLA_VW)
        xs = _merge(xs, mod_s[l], row2(norm_mix[l]), y_sb, y_lru, y_gla, w_gates, wbr, wout,
                    per_token=True, tok_per_seq=tsp, tm=tm_s)
        xs = _ffn(xs, mod_s[l], row2(norm_ffn[l]), wgate, wup, wdown, row2(norm_final),
                  per_token=True, tok_per_seq=tsp, tm=tm_s, final=final)
        outs["ks"].append(k.reshape(bs, tsp, n_heads, SB_DH)[:, :n_new])
        outs["vs"].append(v.reshape(bs, tsp, n_heads, SB_DH)[:, :n_new])
        outs["cvs"].append(cv2)
        outs["hs"].append(h2.reshape(bs, LRU_W))
        outs["ss"].append(s2.reshape(bs, GLA_H, GLA_DK, GLA_DV))

    y_prompt = xp.reshape(bp, seq, d)
    y_sample = xs.reshape(bs, tsp, d)[:, :n_new]
    st = lambda name: jnp.stack(outs[name])
    rows = lambda slab: jnp.transpose(slab.reshape(depth, bp, n_heads, SB_DH, seq), (0, 1, 4, 2, 3))
    return (y_prompt, y_sample, rows(k_slab), rows(v_slab), st("cvp"), st("hp"), st("sp"),
            st("ks"), st("vs"), st("cvs"), st("hs"), st("ss"))
```
